```python
import jax, jax.numpy as jnp
from jax import lax
import numpy as np

D_MODEL = 1024
BATCH = 8
SEQ = 2048
DEPTH = 2
DEC_BATCH = 128
DEC_SEQ = 1
PAST_LEN = 16384
PAGE_SIZE = 128

N_MEM = 256
D_A = D_MODEL
D_CONF = D_MODEL
D_POOL = D_MODEL
N_POOL_GROUPS = 4
POOL_GROUP = D_POOL // N_POOL_GROUPS
POOL_WINDOWS = (2, 4, 8, 16)
POOL_BUF = max(POOL_WINDOWS) - 1
N_MEM_HEADS = 4
MEM_HEAD_DIM = D_MODEL // N_MEM_HEADS
D_MEM = N_MEM_HEADS * MEM_HEAD_DIM
CONV_A_WIDTH = 3
CONV_B_WIDTH = 31
N_BRANCHES = 4
D_FF = 4 * D_MODEL
PROJ_SIZES = (D_A, D_A, D_A, 2 * D_CONF, D_POOL, D_MEM, N_BRANCHES * D_MODEL)
D_PROJ = D_A * 3 + 2 * D_CONF + D_POOL + D_MEM + N_BRANCHES * D_MODEL
EPS = 1e-6

kernel_name = "hybrid_gated_conv_pool_memattn_decoder_step"


def _split_points():
    pts, acc = [], 0
    for s in PROJ_SIZES[:-1]:
        acc += s
        pts.append(acc)
    return pts


def _rmsnorm(x, g):
    xf = x.astype(jnp.float32)
    y = xf * lax.rsqrt(jnp.mean(xf * xf, axis=-1, keepdims=True) + EPS)
    return (y * g.astype(jnp.float32)).astype(x.dtype)


def _layernorm(x, g, b):
    xf = x.astype(jnp.float32)
    mu = jnp.mean(xf, axis=-1, keepdims=True)
    xc = xf - mu
    var = jnp.mean(xc * xc, axis=-1, keepdims=True)
    y = xc * lax.rsqrt(var + EPS) * g.astype(jnp.float32) + b.astype(jnp.float32)
    return y.astype(x.dtype)


def _causal_dwconv(ext, w):
    return lax.conv_general_dilated(
        ext, w[:, None, :].astype(ext.dtype), window_strides=(1,), padding="VALID",
        dimension_numbers=("NWC", "WIO", "NWC"), feature_group_count=ext.shape[-1])


def _multiscale_pool(ext, start_pos):
    b, n, c = ext.shape
    L = n - POOL_BUF
    cs = jnp.cumsum(ext.astype(jnp.float32), axis=1)
    cs0 = jnp.concatenate([jnp.zeros((b, 1, c), jnp.float32), cs], axis=1)
    pos = start_pos + jnp.arange(L)
    outs = []
    for g, w in enumerate(POOL_WINDOWS):
        lo, hi = g * POOL_GROUP, (g + 1) * POOL_GROUP
        s = cs0[:, POOL_BUF + 1:, lo:hi] - cs0[:, POOL_BUF + 1 - w:POOL_BUF + 1 - w + L, lo:hi]
        cnt = jnp.minimum(pos + 1, w).astype(jnp.float32)[None, :, None]
        outs.append(s / cnt)
    mean = jnp.concatenate(outs, axis=-1)
    return (mean - ext[:, POOL_BUF:].astype(jnp.float32)).astype(ext.dtype)


def _mixing_layer(xn, mem_k, mem_v, buf_a, buf_b, buf_pool, start_pos, w_in, conv_a_w,
                  conv_b_w, conv_b_bias, ln_b_gain, ln_b_bias, pool_w, pool_scale,
                  gate_bias, w_o):
    b, L = xn.shape[0], xn.shape[1]
    proj = jnp.einsum("bsd,de->bse", xn, w_in)
    h_a, b_a, c_a, glu_in, p_in, q, gate_logits = jnp.split(proj, _split_points(), axis=-1)
    ext_a = jnp.concatenate([buf_a, c_a * h_a], axis=1)
    y_a = b_a * _causal_dwconv(ext_a, conv_a_w)
    glu = glu_in[..., :D_CONF] * jax.nn.sigmoid(glu_in[..., D_CONF:])
    ext_b = jnp.concatenate([buf_b, glu], axis=1)
    z = _causal_dwconv(ext_b, conv_b_w) + conv_b_bias
    y_b = jax.nn.silu(_layernorm(z, ln_b_gain, ln_b_bias))
    ext_p = jnp.concatenate([buf_pool, p_in], axis=1)
    pooled = _multiscale_pool(ext_p, start_pos).reshape(b, L, N_POOL_GROUPS, POOL_GROUP)
    y_c = jnp.einsum("bsgc,gcd->bsgd", pooled, pool_w).reshape(b, L, D_POOL) * pool_scale
    qh = q.reshape(b, L, N_MEM_HEADS, MEM_HEAD_DIM)
    s = jnp.einsum("bqhd,bkhd->bhqk", qh, mem_k).astype(jnp.float32) * (MEM_HEAD_DIM ** -0.5)
    pr = jax.nn.softmax(s, axis=-1).astype(mem_v.dtype)
    y_m = jnp.einsum("bhqk,bkhd->bqhd", pr, mem_v).reshape(b, L, D_MEM)
    g = jax.nn.sigmoid(gate_logits + gate_bias).reshape(b, L, N_BRANCHES, D_MODEL)
    merged = g[:, :, 0] * y_a + g[:, :, 1] * y_b + g[:, :, 2] * y_c + g[:, :, 3] * y_m
    out = jnp.einsum("bsd,de->bse", merged, w_o)
    return (out, ext_a[:, -(CONV_A_WIDTH - 1):], ext_b[:, -(CONV_B_WIDTH - 1):],
            ext_p[:, -POOL_BUF:])


def _trunk(x, mem_k, mem_v, buf_a, buf_b, buf_pool, start_pos, norm_mix, w_in, conv_a_w,
           conv_b_w, conv_b_bias, ln_b_gain, ln_b_bias, pool_w, pool_scale, gate_bias, w_o,
           norm_ffn, w_ff1, w_ff2, norm_final):
    new_a, new_b, new_p = [], [], []
    for l in range(DEPTH):
        xn = _rmsnorm(x, norm_mix[l])
        mix, na, nb, npool = _mixing_layer(
            xn, mem_k[l], mem_v[l], buf_a[l], buf_b[l], buf_pool[l], start_pos, w_in[l],
            conv_a_w[l], conv_b_w[l], conv_b_bias[l], ln_b_gain[l], ln_b_bias[l], pool_w[l],
            pool_scale[l], gate_bias[l], w_o[l])
        x = x + mix
        xn = _rmsnorm(x, norm_ffn[l])
        h = jnp.square(jax.nn.relu(jnp.einsum("bsd,df->bsf", xn, w_ff1[l])))
        x = x + jnp.einsum("bsf,fd->bsd", h, w_ff2[l])
        new_a.append(na)
        new_b.append(nb)
        new_p.append(npool)
    return (_rmsnorm(x, norm_final), jnp.stack(new_a), jnp.stack(new_b), jnp.stack(new_p))


def setup_inputs(seed: int = 0) -> dict:
    key = jax.random.key(seed)
    ks = jax.random.split(key, 32)
    f32 = jnp.float32
    nrm = lambda k, shape, scale=1.0: (jax.random.normal(k, shape, f32) * scale)
    gain = lambda k, shape: 1.0 + 0.05 * jax.random.normal(k, shape, f32)
    return {
        "x_prompt": nrm(ks[0], (BATCH, SEQ, D_MODEL)),
        "x_sample": nrm(ks[1], (DEC_BATCH, DEC_SEQ, D_MODEL)),
        "mem_prompt": nrm(ks[2], (BATCH, N_MEM, D_MODEL)),
        "cache_mem_k": nrm(ks[3], (DEPTH, DEC_BATCH, N_MEM, N_MEM_HEADS, MEM_HEAD_DIM)),
        "cache_mem_v": nrm(ks[4], (DEPTH, DEC_BATCH, N_MEM, N_MEM_HEADS, MEM_HEAD_DIM)),
        "state_conv_a": nrm(ks[5], (DEPTH, DEC_BATCH, CONV_A_WIDTH - 1, D_A)),
        "state_conv_b": nrm(ks[6], (DEPTH, DEC_BATCH, CONV_B_WIDTH - 1, D_CONF), 0.5),
        "state_pool": nrm(ks[7], (DEPTH, DEC_BATCH, POOL_BUF, D_POOL)),
        "norm_mix": gain(ks[8], (DEPTH, D_MODEL)),
        "norm_mem": gain(ks[9], (DEPTH, D_MODEL)),
        "w_kv": nrm(ks[10], (DEPTH, D_MODEL, 2 * D_MEM), D_MODEL ** -0.5),
        "w_in": nrm(ks[11], (DEPTH, D_MODEL, D_PROJ), D_MODEL ** -0.5),
        "conv_a_w": nrm(ks[12], (DEPTH, CONV_A_WIDTH, D_A), CONV_A_WIDTH ** -0.5),
        "conv_b_w": nrm(ks[13], (DEPTH, CONV_B_WIDTH, D_CONF), CONV_B_WIDTH ** -0.5),
        "conv_b_bias": nrm(ks[14], (DEPTH, D_CONF), 0.02),
        "ln_b_gain": gain(ks[15], (DEPTH, D_CONF)),
        "ln_b_bias": nrm(ks[16], (DEPTH, D_CONF), 0.02),
        "pool_w": nrm(ks[17], (DEPTH, N_POOL_GROUPS, POOL_GROUP, POOL_GROUP), POOL_GROUP ** -0.5),
        "pool_scale": gain(ks[18], (DEPTH, D_POOL)),
        "gate_bias": nrm(ks[19], (DEPTH, N_BRANCHES * D_MODEL), 0.02),
        "w_o": nrm(ks[20], (DEPTH, D_MODEL, D_MODEL), D_MODEL ** -0.5),
        "norm_ffn": gain(ks[21], (DEPTH, D_MODEL)),
        "w_ff1": nrm(ks[22], (DEPTH, D_MODEL, D_FF), D_MODEL ** -0.5),
        "w_ff2": nrm(ks[23], (DEPTH, D_FF, D_MODEL), D_FF ** -0.5),
        "norm_final": gain(ks[24], (D_MODEL,)),
    }


def reference(x_prompt, x_sample, mem_prompt, cache_mem_k, cache_mem_v, state_conv_a,
              state_conv_b, state_pool, norm_mix, norm_mem, w_kv, w_in, conv_a_w, conv_b_w,
              conv_b_bias, ln_b_gain, ln_b_bias, pool_w, pool_scale, gate_bias, w_o,
              norm_ffn, w_ff1, w_ff2, norm_final):
    dt = x_prompt.dtype
    ks, vs = [], []
    for l in range(DEPTH):
        memn = _rmsnorm(mem_prompt, norm_mem[l])
        kv = jnp.einsum("bmd,de->bme", memn, w_kv[l])
        ks.append(kv[..., :D_MEM].reshape(BATCH, N_MEM, N_MEM_HEADS, MEM_HEAD_DIM))
        vs.append(kv[..., D_MEM:].reshape(BATCH, N_MEM, N_MEM_HEADS, MEM_HEAD_DIM))
    mem_k_prompt = jnp.stack(ks)
    mem_v_prompt = jnp.stack(vs)
    weights = (norm_mix, w_in, conv_a_w, conv_b_w, conv_b_bias, ln_b_gain, ln_b_bias, pool_w,
               pool_scale, gate_bias, w_o, norm_ffn, w_ff1, w_ff2, norm_final)
    zeros_a = jnp.zeros((DEPTH, BATCH, CONV_A_WIDTH - 1, D_A), dt)
    zeros_b = jnp.zeros((DEPTH, BATCH, CONV_B_WIDTH - 1, D_CONF), dt)
    zeros_p = jnp.zeros((DEPTH, BATCH, POOL_BUF, D_POOL), dt)
    y_prompt, conv_a_prompt, conv_b_prompt, pool_prompt = _trunk(
        x_prompt, mem_k_prompt, mem_v_prompt, zeros_a, zeros_b, zeros_p, 0, *weights)
    y_sample, conv_a_sample, conv_b_sample, pool_sample = _trunk(
        x_sample, cache_mem_k, cache_mem_v, state_conv_a, state_conv_b, state_pool, PAST_LEN,
        *weights)
    return (y_prompt, y_sample, mem_k_prompt, mem_v_prompt, conv_a_prompt, conv_b_prompt,
            pool_prompt, conv_a_sample, conv_b_sample, pool_sample)
```

```python
import functools

import jax
import jax.numpy as jnp
from jax import lax
from jax.experimental import pallas as pl
from jax.experimental.pallas import tpu as pltpu

D = 1024
DEPTH = 2
N_MEM = 256
N_HEADS = 4
HEAD = D // N_HEADS
N_GROUPS = 4
GROUP = D // N_GROUPS
POOL_WINDOWS = (2, 4, 8, 16)
POOL_BUF = 15
KA = 3
KB = 31
D_FF = 4 * D
D_PROJ = 11 * D
PAST_LEN = 16384
EPS = 1e-6

C_HA, C_BA, C_CA, C_GLU_V, C_GLU_G, C_P, C_Q, C_GATE = (
    0, D, 2 * D, 3 * D, 4 * D, 5 * D, 6 * D, 7 * D)

SUBLANES = 8
LANES = 128
VMEM_LIMIT = 60 * 1024 * 1024

T_MIX = 256
T_FFN = 512
HALO_A, HALO_B, HALO_P = 8, 32, 16

BF16 = jnp.bfloat16
F32 = jnp.float32


def _dot(a, b):
    return jnp.dot(a, b, preferred_element_type=F32)


def _rms_rows(x, g):
    ms = jnp.mean(x * x, axis=-1, keepdims=True)
    return x * lax.rsqrt(ms + EPS) * g


def _shift_rows(v, s, halo, rows):
    return v[halo - s:halo - s + rows]


def _kv_kernel(mem_ref, g_ref, w_ref, k_ref, v_ref, kb_ref, vb_ref):
    memn = _rms_rows(mem_ref[0], g_ref[0]).astype(BF16)
    kv = _dot(memn, w_ref[0])
    k = kv[:, :D]
    v = kv[:, D:]
    k_ref[0, 0] = k
    v_ref[0, 0] = v
    kb_ref[0, 0] = k.astype(BF16)
    vb_ref[0, 0] = v.astype(BF16)


def _kv_call(mem, norm_mem, w_kv_b):
    nb = mem.shape[0]
    out_f = jax.ShapeDtypeStruct((DEPTH, nb, N_MEM, D), F32)
    out_b = jax.ShapeDtypeStruct((DEPTH, nb, N_MEM, D), BF16)
    blk = pl.BlockSpec((1, 1, N_MEM, D), lambda l, b: (l, b, 0, 0))
    return pl.pallas_call(
        _kv_kernel,
        grid=(DEPTH, nb),
        in_specs=[pl.BlockSpec((1, N_MEM, D), lambda l, b: (b, 0, 0)),
                  pl.BlockSpec((1, 1, D), lambda l, b: (l, 0, 0)),
                  pl.BlockSpec((1, D, 2 * D), lambda l, b: (l, 0, 0))],
        out_specs=[blk, blk, blk, blk],
        out_shape=[out_f, out_f, out_b, out_b],
        compiler_params=pltpu.CompilerParams(
            dimension_semantics=("arbitrary", "arbitrary"), vmem_limit_bytes=VMEM_LIMIT),
        name="kv_proj",
    )(mem, norm_mem.reshape(DEPTH, 1, D), w_kv_b)


def _mix_kernel(x_ref, kb_ref, vb_ref, w_in_ref, w_o_ref, pw_ref, nmix_ref, caw_ref, cbw_ref,
                cbb_ref, lng_ref, lnb_ref, psc_ref, gb_ref,
                x1_ref, ta_ref, tb_ref, tp_ref,
                xn_ref, proj_ref, ext_a, ext_b, ext_p, z_ref, mg_ref, mgb_ref, pooled_ref, p_ref):
    T = T_MIX
    j = pl.program_id(1)

    @pl.when(j == 0)
    def _():
        ext_a[0:HALO_A, :] = jnp.zeros((HALO_A, D), F32)
        ext_b[0:HALO_B, :] = jnp.zeros((HALO_B, D), F32)
        ext_p[0:HALO_P, :] = jnp.zeros((HALO_P, D), F32)

    g_mix = nmix_ref[...]
    for r0 in range(0, T, 32):
        xn_ref[r0:r0 + 32, :] = _rms_rows(x_ref[0, r0:r0 + 32, :], g_mix).astype(BF16)

    def proj(c0, c1, dst=None, dst_rows=0):
        r = _dot(xn_ref[...], w_in_ref[:, c0:c1])
        if dst is None:
            proj_ref[:, c0:c1] = r
        else:
            dst[dst_rows:dst_rows + T, :] = r

    proj(C_GLU_V, C_P)
    proj(C_HA, C_GLU_V)
    proj(C_P, C_Q, ext_p, HALO_P)
    proj(C_Q, C_GATE)
    proj(C_GATE, D_PROJ)

    def gate(i, rows, cols=slice(0, D)):
        c = slice(C_GATE + i * D + cols.start, C_GATE + i * D + cols.stop)
        bc = slice(i * D + cols.start, i * D + cols.stop)
        return jax.nn.sigmoid(proj_ref[rows, c] + gb_ref[:, bc])

    for r0 in range(0, T, 32):
        rows = slice(r0, r0 + 32)
        glu = proj_ref[rows, C_GLU_V:C_GLU_G] * jax.nn.sigmoid(proj_ref[rows, C_GLU_G:C_P])
        ext_b[HALO_B + r0:HALO_B + r0 + 32, :] = glu
    R = 64
    for r0 in range(0, T, R):
        for c0 in range(0, D, LANES):
            cols = slice(c0, c0 + LANES)
            acc = None
            for b in range(SUBLANES):
                q = None
                for a in range(4):
                    d = SUBLANES * a + b
                    if d > KB - 1:
                        continue
                    base = HALO_B + r0 - SUBLANES - SUBLANES * a
                    term = ext_b[base:base + R + SUBLANES, cols] * cbw_ref[KB - 1 - d:KB - d, cols]
                    q = term if q is None else q + term
                sh = _shift_rows(q, b, SUBLANES, R)
                acc = sh if acc is None else acc + sh
            z_ref[r0:r0 + R, cols] = acc + cbb_ref[:, cols]
    for r0 in range(0, T, 16):
        rows = slice(r0, r0 + 16)
        z = z_ref[rows, :]
        mu = jnp.mean(z, axis=-1, keepdims=True)
        zc = z - mu
        var = jnp.mean(zc * zc, axis=-1, keepdims=True)
        y = zc * lax.rsqrt(var + EPS) * lng_ref[...] + lnb_ref[...]
        y = y * jax.nn.sigmoid(y)
        mg_ref[rows, :] = gate(1, rows) * y

    for r0 in range(0, T, 32):
        rows = slice(r0, r0 + 32)
        ext_a[HALO_A + r0:HALO_A + r0 + 32, :] = proj_ref[rows, C_CA:C_GLU_V] * proj_ref[rows, C_HA:C_BA]
    for r0 in range(0, T, 32):
        rows = slice(r0, r0 + 32)
        for c0 in range(0, D, 256):
            cols = slice(c0, c0 + 256)
            e = ext_a[r0:r0 + 32 + HALO_A, cols]
            conv = (e[HALO_A:] * caw_ref[2:3, cols]
                    + _shift_rows(e, 1, HALO_A, 32) * caw_ref[1:2, cols]
                    + _shift_rows(e, 2, HALO_A, 32) * caw_ref[0:1, cols])
            y_a = proj_ref[rows, C_BA + c0:C_BA + c0 + 256] * conv
            mg_ref[rows, cols] += gate(0, rows, cols) * y_a

    for r0 in range(0, T, 32):
        pos1 = j * T + r0 + 1 + lax.broadcasted_iota(jnp.int32, (32, 1), 0)
        for g, w in enumerate(POOL_WINDOWS):
            cols = slice(g * GROUP, (g + 1) * GROUP)
            e = ext_p[r0:r0 + 32 + HALO_P, cols]
            s = e
            step = 1
            while step < w:
                s = s + _roll_down(s, step)
                step *= 2
            cnt = jnp.minimum(pos1, w).astype(F32)
            tok = e[HALO_P:]
            pooled = s[HALO_P:] * (1.0 / cnt) - tok
            pooled_ref[r0:r0 + 32, cols] = pooled.astype(BF16)
    for g in range(N_GROUPS):
        cols = slice(g * GROUP, (g + 1) * GROUP)
        y_c = _dot(pooled_ref[:, cols], pw_ref[g]) * psc_ref[:, cols]
        for r0 in range(0, T, 64):
            rows = slice(r0, r0 + 64)
            mg_ref[rows, cols] += gate(2, rows, cols) * y_c[r0:r0 + 64]

    for h in range(N_HEADS):
        cols = slice(h * HEAD, (h + 1) * HEAD)
        qh = proj_ref[:, C_Q + h * HEAD:C_Q + (h + 1) * HEAD].astype(BF16)
        s = lax.dot_general(qh, kb_ref[0, 0, :, cols], (((1,), (1,)), ((), ())),
                            preferred_element_type=F32) * (HEAD ** -0.5)
        for r0 in range(0, T, 64):
            sc = s[r0:r0 + 64]
            m = jnp.max(sc, axis=-1, keepdims=True)
            e = jnp.exp(sc - m)
            p = e * (1.0 / jnp.sum(e, axis=-1, keepdims=True))
            p_ref[r0:r0 + 64, :] = p.astype(BF16)
        y_m = _dot(p_ref[...], vb_ref[0, 0, :, cols])
        for r0 in range(0, T, 64):
            rows = slice(r0, r0 + 64)
            mg_ref[rows, cols] += gate(3, rows, cols) * y_m[r0:r0 + 64]

    for r0 in range(0, T, 32):
        mgb_ref[r0:r0 + 32, :] = mg_ref[r0:r0 + 32, :].astype(BF16)
    x1_ref[0] = x_ref[0] + _dot(mgb_ref[...], w_o_ref[...])

    ta = ext_a[T:T + HALO_A, :]
    tb = ext_b[T:T + HALO_B, :]
    tp = ext_p[T:T + HALO_P, :]
    ta_ref[0] = ta
    tb_ref[0] = tb
    tp_ref[0] = tp
    ext_a[0:HALO_A, :] = ta
    ext_b[0:HALO_B, :] = tb
    ext_p[0:HALO_P, :] = tp


def _roll_down(v, s):
    return pltpu.roll(v, s, axis=0)


def _const_spec(shape):
    nd = len(shape)
    return pl.BlockSpec(shape, lambda b, j: (0,) * nd, pipeline_mode=pl.Buffered(1))


def _mix_call(l, x, kb, vb, w_in_b, w_o_b, pw_b, nmix, caw, cbw, cbb, lng, lnb, psc, gb):
    nb, seq, _ = x.shape
    T = T_MIX
    vec = lambda a: a.reshape(1, -1)
    kv_spec = pl.BlockSpec((1, 1, N_MEM, D), lambda b, j: (l, b, 0, 0))
    return pl.pallas_call(
        _mix_kernel,
        grid=(nb, seq // T),
        in_specs=[pl.BlockSpec((1, T, D), lambda b, j: (b, j, 0)),
                  kv_spec, kv_spec,
                  _const_spec((D, D_PROJ)), _const_spec((D, D)), _const_spec((N_GROUPS, GROUP, GROUP)),
                  _const_spec((1, D)), _const_spec((KA, D)), _const_spec((KB, D)),
                  _const_spec((1, D)), _const_spec((1, D)), _const_spec((1, D)), _const_spec((1, D)),
                  _const_spec((1, 4 * D))],
        out_specs=[pl.BlockSpec((1, T, D), lambda b, j: (b, j, 0)),
                   pl.BlockSpec((1, HALO_A, D), lambda b, j: (b, 0, 0)),
                   pl.BlockSpec((1, HALO_B, D), lambda b, j: (b, 0, 0)),
                   pl.BlockSpec((1, HALO_P, D), lambda b, j: (b, 0, 0))],
        out_shape=[jax.ShapeDtypeStruct((nb, seq, D), F32),
                   jax.ShapeDtypeStruct((nb, HALO_A, D), F32),
                   jax.ShapeDtypeStruct((nb, HALO_B, D), F32),
                   jax.ShapeDtypeStruct((nb, HALO_P, D), F32)],
        scratch_shapes=[pltpu.VMEM((T, D), BF16),
                        pltpu.VMEM((T, D_PROJ), F32),
                        pltpu.VMEM((HALO_A + T, D), F32),
                        pltpu.VMEM((HALO_B + T, D), F32),
                        pltpu.VMEM((HALO_P + T, D), F32),
                        pltpu.VMEM((T, D), F32),
                        pltpu.VMEM((T, D), F32),
                        pltpu.VMEM((T, D), BF16),
                        pltpu.VMEM((T, D), BF16),
                        pltpu.VMEM((T, N_MEM), BF16)],
        compiler_params=pltpu.CompilerParams(
            dimension_semantics=("arbitrary", "arbitrary"), vmem_limit_bytes=VMEM_LIMIT),
        name=f"mix_prompt_l{l}",
    )(x, kb, vb, w_in_b, w_o_b, pw_b, vec(nmix), caw, cbw, vec(cbb), vec(lng), vec(lnb), vec(psc),
      vec(gb))


def _ffn_kernel(*refs, tile, with_merge, final_norm):
    R = 32
    if with_merge:
        (x_ref, part_ref, g3_ref, ym_ref, w_o_ref, nffn_ref, w1_ref, w2_ref, nfin_ref,
         o_ref, xn_ref, h_ref, x1_ref) = refs
        for r0 in range(0, tile, R):
            rows = slice(r0, r0 + R)
            xn_ref[rows, :] = (part_ref[rows, :] + g3_ref[rows, :] * ym_ref[rows, :]).astype(BF16)
        x1_ref[...] = x_ref[...] + _dot(xn_ref[...], w_o_ref[...])
    else:
        (x1_ref, nffn_ref, w1_ref, w2_ref, nfin_ref, o_ref, xn_ref, h_ref) = refs
    g_ffn = nffn_ref[...]
    for r0 in range(0, tile, R):
        rows = slice(r0, r0 + R)
        xn_ref[rows, :] = _rms_rows(x1_ref[rows, :], g_ffn).astype(BF16)
    for f0 in range(0, D_FF, D):
        h = jnp.maximum(_dot(xn_ref[...], w1_ref[:, f0:f0 + D]), 0.0)
        h_ref[:, f0:f0 + D] = (h * h).astype(BF16)
    y = x1_ref[...] + _dot(h_ref[...], w2_ref[...])
    if final_norm:
        y = _rms_rows(y, nfin_ref[...])
    o_ref[...] = y


def _ffn_call(x, nffn, w1_b, w2_b, nfin, *, final_norm, merge=None, name):
    n = x.shape[0]
    tile = min(T_FFN, n)
    row = pl.BlockSpec((tile, D), lambda i: (i, 0))
    const = lambda shape: pl.BlockSpec(shape, lambda i: (0,) * len(shape), pipeline_mode=pl.Buffered(1))
    args, specs = [x], [row]
    if merge is not None:
        part, g3, ym, w_o_b = merge
        args += [part, g3, ym, w_o_b]
        specs += [row, row, row, const((D, D))]
    args += [nffn.reshape(1, D), w1_b, w2_b, nfin.reshape(1, D)]
    specs += [const((1, D)), const((D, D_FF)), const((D_FF, D)), const((1, D))]
    return pl.pallas_call(
        functools.partial(_ffn_kernel, tile=tile, with_merge=merge is not None, final_norm=final_norm),
        grid=(n // tile,),
        in_specs=specs,
        out_specs=row,
        out_shape=jax.ShapeDtypeStruct((n, D), F32),
        scratch_shapes=[pltpu.VMEM((tile, D), BF16), pltpu.VMEM((tile, D_FF), BF16)]
        + ([pltpu.VMEM((tile, D), F32)] if merge is not None else []),
        compiler_params=pltpu.CompilerParams(
            dimension_semantics=("arbitrary",), vmem_limit_bytes=VMEM_LIMIT),
        name=name,
    )(*args)


def _state_reduce_kernel(st_ref, wt_ref, o_ref, *, bt):
    wt = wt_ref[0]
    for i in range(bt):
        o_ref[0, i:i + 1, :] = jnp.sum(st_ref[0, i] * wt, axis=0, keepdims=True)


def _state_reduce_call(state, wt, name):
    _, n, k, _ = state.shape
    bt = 16
    return pl.pallas_call(
        functools.partial(_state_reduce_kernel, bt=bt),
        grid=(DEPTH, n // bt),
        in_specs=[pl.BlockSpec((1, bt, k, D), lambda l, i: (l, i, 0, 0)),
                  pl.BlockSpec((1, k, D), lambda l, i: (l, 0, 0))],
        out_specs=pl.BlockSpec((1, bt, D), lambda l, i: (l, i, 0)),
        out_shape=jax.ShapeDtypeStruct((DEPTH, n, D), F32),
        compiler_params=pltpu.CompilerParams(
            dimension_semantics=("arbitrary", "arbitrary"), vmem_limit_bytes=VMEM_LIMIT),
        name=name,
    )(state, wt)


def _mix_sample_kernel(x_ref, za_ref, zb_ref, sp_ref, w_in_ref, pw_ref, nmix_ref, caw_ref, cbw_ref,
                       cbb_ref, lng_ref, lnb_ref, psc_ref, gb_ref,
                       part_ref, q_ref, g3_ref, u_ref, glu_ref, pin_ref,
                       xn_ref, proj_ref, pooled_ref, *, n):
    R = 32
    g_mix = nmix_ref[...]
    for r0 in range(0, n, R):
        xn_ref[r0:r0 + R, :] = _rms_rows(x_ref[r0:r0 + R, :], g_mix).astype(BF16)
    for c0 in range(0, D_PROJ, D):
        proj_ref[:, c0:c0 + D] = _dot(xn_ref[...], w_in_ref[:, c0:c0 + D])

    def gate(i, rows, cols=slice(0, D)):
        c = slice(C_GATE + i * D + cols.start, C_GATE + i * D + cols.stop)
        bc = slice(i * D + cols.start, i * D + cols.stop)
        return jax.nn.sigmoid(proj_ref[rows, c] + gb_ref[:, bc])

    cnt = [float(min(PAST_LEN + 1, w)) for w in POOL_WINDOWS]
    for r0 in range(0, n, R):
        rows = slice(r0, r0 + R)
        u = proj_ref[rows, C_CA:C_GLU_V] * proj_ref[rows, C_HA:C_BA]
        u_ref[rows, :] = u
        y_a = proj_ref[rows, C_BA:C_CA] * (za_ref[0, rows, :] + u * caw_ref[KA - 1:KA, :])
        merged = gate(0, rows) * y_a
        glu = proj_ref[rows, C_GLU_V:C_GLU_G] * jax.nn.sigmoid(proj_ref[rows, C_GLU_G:C_P])
        glu_ref[rows, :] = glu
        z = zb_ref[0, rows, :] + glu * cbw_ref[KB - 1:KB, :] + cbb_ref[...]
        mu = jnp.mean(z, axis=-1, keepdims=True)
        zc = z - mu
        var = jnp.mean(zc * zc, axis=-1, keepdims=True)
        y = zc * lax.rsqrt(var + EPS) * lng_ref[...] + lnb_ref[...]
        y = y * jax.nn.sigmoid(y)
        merged = merged + gate(1, rows) * y
        part_ref[rows, :] = merged
        p_in = proj_ref[rows, C_P:C_Q]
        pin_ref[rows, :] = p_in
        s = sp_ref[0, rows, :] + p_in
        for g in range(N_GROUPS):
            cols = slice(g * GROUP, (g + 1) * GROUP)
            pooled_ref[rows, cols] = (s[:, cols] * (1.0 / cnt[g]) - p_in[:, cols]).astype(BF16)
        q_ref[rows, :] = proj_ref[rows, C_Q:C_GATE]
        g3_ref[rows, :] = gate(3, rows)
    for g in range(N_GROUPS):
        cols = slice(g * GROUP, (g + 1) * GROUP)
        y_c = _dot(pooled_ref[:, cols], pw_ref[g]) * psc_ref[:, cols]
        for r0 in range(0, n, R):
            rows = slice(r0, r0 + R)
            part_ref[rows, cols] += gate(2, rows, cols) * y_c[r0:r0 + R]


def _mix_sample_call(l, x, za, zb, sp, w_in_b, pw_b, nmix, caw, cbw, cbb, lng, lnb, psc, gb):
    n = x.shape[0]
    vec = lambda a: a.reshape(1, -1)
    const = lambda shape: pl.BlockSpec(shape, lambda i: (0,) * len(shape), pipeline_mode=pl.Buffered(1))
    lay = pl.BlockSpec((1, n, D), lambda i: (l, 0, 0))
    row = pl.BlockSpec((n, D), lambda i: (0, 0))
    out = jax.ShapeDtypeStruct((n, D), F32)
    return pl.pallas_call(
        functools.partial(_mix_sample_kernel, n=n),
        grid=(1,),
        in_specs=[row, lay, lay, lay,
                  const((D, D_PROJ)), const((N_GROUPS, GROUP, GROUP)),
                  const((1, D)), const((KA, D)), const((KB, D)),
                  const((1, D)), const((1, D)), const((1, D)), const((1, D)), const((1, 4 * D))],
        out_specs=[row] * 6,
        out_shape=[out] * 6,
        scratch_shapes=[pltpu.VMEM((n, D), BF16), pltpu.VMEM((n, D_PROJ), F32),
                        pltpu.VMEM((n, D), BF16)],
        compiler_params=pltpu.CompilerParams(
            dimension_semantics=("arbitrary",), vmem_limit_bytes=VMEM_LIMIT),
        name=f"mix_sample_l{l}",
    )(x, za, zb, sp, w_in_b, pw_b, vec(nmix), caw, cbw, vec(cbb), vec(lng), vec(lnb), vec(psc), vec(gb))


def _attn_sample_kernel(q_ref, k_ref, v_ref, o_ref, *, bt):
    for i in range(bt):
        for h in range(N_HEADS):
            cols = slice(h * HEAD, (h + 1) * HEAD)
            qh = q_ref[i:i + 1, cols]
            s = jnp.sum(k_ref[0, i, :, cols] * qh, axis=-1, keepdims=True) * (HEAD ** -0.5)
            m = jnp.max(s, axis=0, keepdims=True)
            e = jnp.exp(s - m)
            p = e * (1.0 / jnp.sum(e, axis=0, keepdims=True))
            o_ref[i:i + 1, cols] = jnp.sum(p * v_ref[0, i, :, cols], axis=0, keepdims=True)


def _attn_sample_call(l, q, cache_k, cache_v):
    n = q.shape[0]
    bt = 8
    kv = pl.BlockSpec((1, bt, N_MEM, D), lambda i: (l, i, 0, 0))
    return pl.pallas_call(
        functools.partial(_attn_sample_kernel, bt=bt),
        grid=(n // bt,),
        in_specs=[pl.BlockSpec((bt, D), lambda i: (i, 0)), kv, kv],
        out_specs=pl.BlockSpec((bt, D), lambda i: (i, 0)),
        out_shape=jax.ShapeDtypeStruct((n, D), F32),
        compiler_params=pltpu.CompilerParams(
            dimension_semantics=("arbitrary",), vmem_limit_bytes=VMEM_LIMIT),
        name=f"attn_sample_l{l}",
    )(q, cache_k, cache_v)


def kernel(x_prompt, x_sample, mem_prompt, cache_mem_k, cache_mem_v, state_conv_a, state_conv_b,
           state_pool, norm_mix, norm_mem, w_kv, w_in, conv_a_w, conv_b_w, conv_b_bias, ln_b_gain,
           ln_b_bias, pool_w, pool_scale, gate_bias, w_o, norm_ffn, w_ff1, w_ff2, norm_final):
    nb, seq, _ = x_prompt.shape
    ns = x_sample.shape[0]

    w_kv_b = w_kv.astype(BF16)
    w_in_b = w_in.astype(BF16)
    w_o_b = w_o.astype(BF16)
    pw_b = pool_w.astype(BF16)
    w1_b = w_ff1.astype(BF16)
    w2_b = w_ff2.astype(BF16)

    mem_k, mem_v, kb, vb = _kv_call(mem_prompt, norm_mem, w_kv_b)
    x = x_prompt
    tails_a, tails_b, tails_p = [], [], []
    for l in range(DEPTH):
        x, ta, tb, tp = _mix_call(l, x, kb, vb, w_in_b[l], w_o_b[l], pw_b[l], norm_mix[l],
                                  conv_a_w[l], conv_b_w[l], conv_b_bias[l], ln_b_gain[l],
                                  ln_b_bias[l], pool_scale[l], gate_bias[l])
        x = _ffn_call(x.reshape(nb * seq, D), norm_ffn[l], w1_b[l], w2_b[l], norm_final,
                      final_norm=(l == DEPTH - 1), name=f"ffn_prompt_l{l}").reshape(nb, seq, D)
        tails_a.append(ta[:, HALO_A - (KA - 1):])
        tails_b.append(tb[:, HALO_B - (KB - 1):])
        tails_p.append(tp[:, HALO_P - POOL_BUF:])
    y_prompt = x
    shape_kv = (DEPTH, nb, N_MEM, N_HEADS, HEAD)
    mem_k_prompt = mem_k.reshape(shape_kv)
    mem_v_prompt = mem_v.reshape(shape_kv)
    conv_a_prompt = jnp.stack(tails_a)
    conv_b_prompt = jnp.stack(tails_b)
    pool_prompt = jnp.stack(tails_p)

    pool_mask = jnp.concatenate(
        [jnp.broadcast_to((jnp.arange(POOL_BUF) >= POOL_BUF + 1 - w).astype(F32)[:, None], (POOL_BUF, GROUP))
         for w in POOL_WINDOWS], axis=1)
    za = _state_reduce_call(state_conv_a, conv_a_w[:, :KA - 1], "state_conv_a")
    zb = _state_reduce_call(state_conv_b, conv_b_w[:, :KB - 1], "state_conv_b")
    sp = _state_reduce_call(state_pool, jnp.broadcast_to(pool_mask, (DEPTH, POOL_BUF, D)), "state_pool")
    ck = cache_mem_k.reshape(DEPTH, ns, N_MEM, D)
    cv = cache_mem_v.reshape(DEPTH, ns, N_MEM, D)
    xs = x_sample.reshape(ns, D)
    new_a, new_b, new_p = [], [], []
    for l in range(DEPTH):
        part, q, g3, u, glu, p_in = _mix_sample_call(
            l, xs, za, zb, sp, w_in_b[l], pw_b[l], norm_mix[l], conv_a_w[l], conv_b_w[l],
            conv_b_bias[l], ln_b_gain[l], ln_b_bias[l], pool_scale[l], gate_bias[l])
        ym = _attn_sample_call(l, q, ck, cv)
        xs = _ffn_call(xs, norm_ffn[l], w1_b[l], w2_b[l], norm_final, final_norm=(l == DEPTH - 1),
                       merge=(part, g3, ym, w_o_b[l]), name=f"ffn_sample_l{l}")
        new_a.append(jnp.concatenate([state_conv_a[l, :, 1:], u[:, None]], axis=1))
        new_b.append(jnp.concatenate([state_conv_b[l, :, 1:], glu[:, None]], axis=1))
        new_p.append(jnp.concatenate([state_pool[l, :, 1:], p_in[:, None]], axis=1))
    y_sample = xs.reshape(ns, 1, D)
    conv_a_sample = jnp.stack(new_a)
    conv_b_sample = jnp.stack(new_b)
    pool_sample = jnp.stack(new_p)

    return (y_prompt, y_sample, mem_k_prompt, mem_v_prompt, conv_a_prompt, conv_b_prompt,
            pool_prompt, conv_a_sample, conv_b_sample, pool_sample)
```

```python
import functools

import jax
import jax.numpy as jnp
from jax import lax
from jax.experimental import pallas as pl
from jax.experimental.pallas import tpu as pltpu

D = 1024
DEPTH = 2
N_MEM = 256
N_HEADS = 4
HEAD = D // N_HEADS
N_GROUPS = 4
GROUP = D // N_GROUPS
POOL_WINDOWS = (2, 4, 8, 16)
POOL_BUF = 15
KA = 3
KB = 31
D_FF = 4 * D
D_PROJ = 11 * D
PAST_LEN = 16384
EPS = 1e-6

C_HA, C_BA, C_CA, C_GLU_V, C_GLU_G, C_P, C_Q, C_GATE = (
    0, D, 2 * D, 3 * D, 4 * D, 5 * D, 6 * D, 7 * D)

SUBLANES = 8
LANES = 128
VMEM_LIMIT = 60 * 1024 * 1024

T_MIX = 256
T_FFN = 512
HALO_A, HALO_B, HALO_P = 8, 32, 16

BF16 = jnp.bfloat16
F32 = jnp.float32


def _dot(a, b):
    return jnp.dot(a, b, preferred_element_type=F32)


def _rms_rows(x, g):
    ms = jnp.mean(x * x, axis=-1, keepdims=True)
    return x * lax.rsqrt(ms + EPS) * g


def _shift_rows(v, s, halo, rows):
    return v[halo - s:halo - s + rows]


def _kv_kernel(mem_ref, g_ref, w_ref, k_ref, v_ref, kb_ref, vb_ref):
    memn = _rms_rows(mem_ref[0], g_ref[0]).astype(BF16)
    kv = _dot(memn, w_ref[0])
    k = kv[:, :D]
    v = kv[:, D:]
    k_ref[0, 0] = k
    v_ref[0, 0] = v
    kb_ref[0, 0] = k.astype(BF16)
    vb_ref[0, 0] = v.astype(BF16)


def _kv_call(mem, norm_mem, w_kv_b):
    nb = mem.shape[0]
    out_f = jax.ShapeDtypeStruct((DEPTH, nb, N_MEM, D), F32)
    out_b = jax.ShapeDtypeStruct((DEPTH, nb, N_MEM, D), BF16)
    blk = pl.BlockSpec((1, 1, N_MEM, D), lambda l, b: (l, b, 0, 0))
    return pl.pallas_call(
        _kv_kernel,
        grid=(DEPTH, nb),
        in_specs=[pl.BlockSpec((1, N_MEM, D), lambda l, b: (b, 0, 0)),
                  pl.BlockSpec((1, 1, D), lambda l, b: (l, 0, 0)),
                  pl.BlockSpec((1, D, 2 * D), lambda l, b: (l, 0, 0))],
        out_specs=[blk, blk, blk, blk],
        out_shape=[out_f, out_f, out_b, out_b],
        compiler_params=pltpu.CompilerParams(
            dimension_semantics=("arbitrary", "arbitrary"), vmem_limit_bytes=VMEM_LIMIT),
        name="kv_proj",
    )(mem, norm_mem.reshape(DEPTH, 1, D), w_kv_b)


def _mix_kernel(x_ref, kb_ref, vb_ref, w_in_ref, w_o_ref, pw_ref, nmix_ref, caw_ref, cbw_ref,
                cbb_ref, lng_ref, lnb_ref, psc_ref, gb_ref,
                x1_ref, ta_ref, tb_ref, tp_ref,
                xn_ref, proj_ref, ext_a, ext_b, ext_p, z_ref, mg_ref, mgb_ref, pooled_ref, p_ref):
    T = T_MIX
    j = pl.program_id(1)

    @pl.when(j == 0)
    def _():
        ext_a[0:HALO_A, :] = jnp.zeros((HALO_A, D), F32)
        ext_b[0:HALO_B, :] = jnp.zeros((HALO_B, D), F32)
        ext_p[0:HALO_P, :] = jnp.zeros((HALO_P, D), F32)

    g_mix = nmix_ref[...]
    for r0 in range(0, T, 32):
        xn_ref[r0:r0 + 32, :] = _rms_rows(x_ref[0, r0:r0 + 32, :], g_mix).astype(BF16)

    def proj(c0, c1, dst=None, dst_rows=0):
        r = _dot(xn_ref[...], w_in_ref[:, c0:c1])
        if dst is None:
            proj_ref[:, c0:c1] = r
        else:
            dst[dst_rows:dst_rows + T, :] = r

    proj(C_GLU_V, C_P)
    proj(C_HA, C_GLU_V)
    proj(C_P, C_Q, ext_p, HALO_P)
    proj(C_Q, C_GATE)
    proj(C_GATE, D_PROJ)

    def gate(i, rows, cols=slice(0, D)):
        c = slice(C_GATE + i * D + cols.start, C_GATE + i * D + cols.stop)
        bc = slice(i * D + cols.start, i * D + cols.stop)
        return jax.nn.sigmoid(proj_ref[rows, c] + gb_ref[:, bc])

    for r0 in range(0, T, 32):
        rows = slice(r0, r0 + 32)
        glu = proj_ref[rows, C_GLU_V:C_GLU_G] * jax.nn.sigmoid(proj_ref[rows, C_GLU_G:C_P])
        ext_b[HALO_B + r0:HALO_B + r0 + 32, :] = glu
    R = 128
    for r0 in range(0, T, R):
        for c0 in range(0, D, LANES):
            cols = slice(c0, c0 + LANES)
            acc = None
            for b in range(SUBLANES):
                q = None
                for a in range(4):
                    d = SUBLANES * a + b
                    if d > KB - 1:
                        continue
                    base = HALO_B + r0 - SUBLANES - SUBLANES * a
                    term = ext_b[base:base + R + SUBLANES, cols] * cbw_ref[KB - 1 - d:KB - d, cols]
                    q = term if q is None else q + term
                sh = _shift_rows(q, b, SUBLANES, R)
                acc = sh if acc is None else acc + sh
            z_ref[r0:r0 + R, cols] = acc + cbb_ref[:, cols]
    for r0 in range(0, T, 16):
        rows = slice(r0, r0 + 16)
        z = z_ref[rows, :]
        mu = jnp.mean(z, axis=-1, keepdims=True)
        zc = z - mu
        var = jnp.mean(zc * zc, axis=-1, keepdims=True)
        y = zc * lax.rsqrt(var + EPS) * lng_ref[...] + lnb_ref[...]
        y = y * jax.nn.sigmoid(y)
        mg_ref[rows, :] = gate(1, rows) * y

    for r0 in range(0, T, 32):
        rows = slice(r0, r0 + 32)
        ext_a[HALO_A + r0:HALO_A + r0 + 32, :] = proj_ref[rows, C_CA:C_GLU_V] * proj_ref[rows, C_HA:C_BA]
    for r0 in range(0, T, 32):
        rows = slice(r0, r0 + 32)
        for c0 in range(0, D, 256):
            cols = slice(c0, c0 + 256)
            e = ext_a[r0:r0 + 32 + HALO_A, cols]
            conv = (e[HALO_A:] * caw_ref[2:3, cols]
                    + _shift_rows(e, 1, HALO_A, 32) * caw_ref[1:2, cols]
                    + _shift_rows(e, 2, HALO_A, 32) * caw_ref[0:1, cols])
            y_a = proj_ref[rows, C_BA + c0:C_BA + c0 + 256] * conv
            mg_ref[rows, cols] += gate(0, rows, cols) * y_a

    for r0 in range(0, T, 32):
        pos1 = j * T + r0 + 1 + lax.broadcasted_iota(jnp.int32, (32, 1), 0)
        for g, w in enumerate(POOL_WINDOWS):
            cols = slice(g * GROUP, (g + 1) * GROUP)
            e = ext_p[r0:r0 + 32 + HALO_P, cols]
            s = e
            step = 1
            while step < w:
                s = s + _roll_down(s, step)
                step *= 2
            cnt = jnp.minimum(pos1, w).astype(F32)
            tok = e[HALO_P:]
            pooled = s[HALO_P:] * (1.0 / cnt) - tok
            pooled_ref[r0:r0 + 32, cols] = pooled.astype(BF16)
    for g in range(N_GROUPS):
        cols = slice(g * GROUP, (g + 1) * GROUP)
        y_c = _dot(pooled_ref[:, cols], pw_ref[g]) * psc_ref[:, cols]
        for r0 in range(0, T, 64):
            rows = slice(r0, r0 + 64)
            mg_ref[rows, cols] += gate(2, rows, cols) * y_c[r0:r0 + 64]

    for h in range(N_HEADS):
        cols = slice(h * HEAD, (h + 1) * HEAD)
        qh = proj_ref[:, C_Q + h * HEAD:C_Q + (h + 1) * HEAD].astype(BF16)
        s = lax.dot_general(qh, kb_ref[0, 0, :, cols], (((1,), (1,)), ((), ())),
                            preferred_element_type=F32) * (HEAD ** -0.5)
        for r0 in range(0, T, 64):
            sc = s[r0:r0 + 64]
            m = jnp.max(sc, axis=-1, keepdims=True)
            e = jnp.exp(sc - m)
            p = e * (1.0 / jnp.sum(e, axis=-1, keepdims=True))
            p_ref[r0:r0 + 64, :] = p.astype(BF16)
        y_m = _dot(p_ref[...], vb_ref[0, 0, :, cols])
        for r0 in range(0, T, 64):
            rows = slice(r0, r0 + 64)
            mg_ref[rows, cols] += gate(3, rows, cols) * y_m[r0:r0 + 64]

    for r0 in range(0, T, 32):
        mgb_ref[r0:r0 + 32, :] = mg_ref[r0:r0 + 32, :].astype(BF16)
    x1_ref[0] = x_ref[0] + _dot(mgb_ref[...], w_o_ref[...])

    ta = ext_a[T:T + HALO_A, :]
    tb = ext_b[T:T + HALO_B, :]
    tp = ext_p[T:T + HALO_P, :]
    ta_ref[0] = ta
    tb_ref[0] = tb
    tp_ref[0] = tp
    ext_a[0:HALO_A, :] = ta
    ext_b[0:HALO_B, :] = tb
    ext_p[0:HALO_P, :] = tp


def _roll_down(v, s):
    return pltpu.roll(v, s, axis=0)


def _const_spec(shape):
    nd = len(shape)
    return pl.BlockSpec(shape, lambda b, j: (0,) * nd, pipeline_mode=pl.Buffered(1))


def _mix_call(l, x, kb, vb, w_in_b, w_o_b, pw_b, nmix, caw, cbw, cbb, lng, lnb, psc, gb):
    nb, seq, _ = x.shape
    T = T_MIX
    vec = lambda a: a.reshape(1, -1)
    kv_spec = pl.BlockSpec((1, 1, N_MEM, D), lambda b, j: (l, b, 0, 0))
    return pl.pallas_call(
        _mix_kernel,
        grid=(nb, seq // T),
        in_specs=[pl.BlockSpec((1, T, D), lambda b, j: (b, j, 0)),
                  kv_spec, kv_spec,
                  _const_spec((D, D_PROJ)), _const_spec((D, D)), _const_spec((N_GROUPS, GROUP, GROUP)),
                  _const_spec((1, D)), _const_spec((KA, D)), _const_spec((KB, D)),
                  _const_spec((1, D)), _const_spec((1, D)), _const_spec((1, D)), _const_spec((1, D)),
                  _const_spec((1, 4 * D))],
        out_specs=[pl.BlockSpec((1, T, D), lambda b, j: (b, j, 0)),
                   pl.BlockSpec((1, HALO_A, D), lambda b, j: (b, 0, 0)),
                   pl.BlockSpec((1, HALO_B, D), lambda b, j: (b, 0, 0)),
                   pl.BlockSpec((1, HALO_P, D), lambda b, j: (b, 0, 0))],
        out_shape=[jax.ShapeDtypeStruct((nb, seq, D), F32),
                   jax.ShapeDtypeStruct((nb, HALO_A, D), F32),
                   jax.ShapeDtypeStruct((nb, HALO_B, D), F32),
                   jax.ShapeDtypeStruct((nb, HALO_P, D), F32)],
        scratch_shapes=[pltpu.VMEM((T, D), BF16),
                        pltpu.VMEM((T, D_PROJ), F32),
                        pltpu.VMEM((HALO_A + T, D), F32),
                        pltpu.VMEM((HALO_B + T, D), F32),
                        pltpu.VMEM((HALO_P + T, D), F32),
                        pltpu.VMEM((T, D), F32),
                        pltpu.VMEM((T, D), F32),
                        pltpu.VMEM((T, D), BF16),
                        pltpu.VMEM((T, D), BF16),
                        pltpu.VMEM((T, N_MEM), BF16)],
        compiler_params=pltpu.CompilerParams(
            dimension_semantics=("arbitrary", "arbitrary"), vmem_limit_bytes=VMEM_LIMIT),
        name=f"mix_prompt_l{l}",
    )(x, kb, vb, w_in_b, w_o_b, pw_b, vec(nmix), caw, cbw, vec(cbb), vec(lng), vec(lnb), vec(psc),
      vec(gb))


def _ffn_kernel(*refs, tile, with_merge, final_norm):
    R = 32
    if with_merge:
        (x_ref, part_ref, g3_ref, ym_ref, w_o_ref, nffn_ref, w1_ref, w2_ref, nfin_ref,
         o_ref, xn_ref, h_ref, x1_ref) = refs
        for r0 in range(0, tile, R):
            rows = slice(r0, r0 + R)
            xn_ref[rows, :] = (part_ref[rows, :] + g3_ref[rows, :] * ym_ref[rows, :]).astype(BF16)
        x1_ref[...] = x_ref[...] + _dot(xn_ref[...], w_o_ref[...])
    else:
        (x1_ref, nffn_ref, w1_ref, w2_ref, nfin_ref, o_ref, xn_ref, h_ref) = refs
    g_ffn = nffn_ref[...]
    for r0 in range(0, tile, R):
        rows = slice(r0, r0 + R)
        xn_ref[rows, :] = _rms_rows(x1_ref[rows, :], g_ffn).astype(BF16)
    for f0 in range(0, D_FF, D):
        h = jnp.maximum(_dot(xn_ref[...], w1_ref[:, f0:f0 + D]), 0.0)
        h_ref[:, f0:f0 + D] = (h * h).astype(BF16)
    y = x1_ref[...] + _dot(h_ref[...], w2_ref[...])
    if final_norm:
        y = _rms_rows(y, nfin_ref[...])
    o_ref[...] = y


def _ffn_call(x, nffn, w1_b, w2_b, nfin, *, final_norm, merge=None, name):
    n = x.shape[0]
    tile = min(T_FFN, n)
    row = pl.BlockSpec((tile, D), lambda i: (i, 0))
    const = lambda shape: pl.BlockSpec(shape, lambda i: (0,) * len(shape), pipeline_mode=pl.Buffered(1))
    args, specs = [x], [row]
    if merge is not None:
        part, g3, ym, w_o_b = merge
        args += [part, g3, ym, w_o_b]
        specs += [row, row, row, const((D, D))]
    args += [nffn.reshape(1, D), w1_b, w2_b, nfin.reshape(1, D)]
    specs += [const((1, D)), const((D, D_FF)), const((D_FF, D)), const((1, D))]
    return pl.pallas_call(
        functools.partial(_ffn_kernel, tile=tile, with_merge=merge is not None, final_norm=final_norm),
        grid=(n // tile,),
        in_specs=specs,
        out_specs=row,
        out_shape=jax.ShapeDtypeStruct((n, D), F32),
        scratch_shapes=[pltpu.VMEM((tile, D), BF16), pltpu.VMEM((tile, D_FF), BF16)]
        + ([pltpu.VMEM((tile, D), F32)] if merge is not None else []),
        compiler_params=pltpu.CompilerParams(
            dimension_semantics=("arbitrary",), vmem_limit_bytes=VMEM_LIMIT),
        name=name,
    )(*args)


def _state_reduce_kernel(st_ref, wt_ref, o_ref, *, bt):
    wt = wt_ref[0]
    for i in range(bt):
        o_ref[0, i:i + 1, :] = jnp.sum(st_ref[0, i] * wt, axis=0, keepdims=True)


def _state_reduce_call(state, wt, name):
    _, n, k, _ = state.shape
    bt = 16
    return pl.pallas_call(
        functools.partial(_state_reduce_kernel, bt=bt),
        grid=(DEPTH, n // bt),
        in_specs=[pl.BlockSpec((1, bt, k, D), lambda l, i: (l, i, 0, 0)),
                  pl.BlockSpec((1, k, D), lambda l, i: (l, 0, 0))],
        out_specs=pl.BlockSpec((1, bt, D), lambda l, i: (l, i, 0)),
        out_shape=jax.ShapeDtypeStruct((DEPTH, n, D), F32),
        compiler_params=pltpu.CompilerParams(
            dimension_semantics=("arbitrary", "arbitrary"), vmem_limit_bytes=VMEM_LIMIT),
        name=name,
    )(state, wt)


def _mix_sample_kernel(x_ref, za_ref, zb_ref, sp_ref, w_in_ref, pw_ref, nmix_ref, caw_ref, cbw_ref,
                       cbb_ref, lng_ref, lnb_ref, psc_ref, gb_ref,
                       part_ref, q_ref, g3_ref, u_ref, glu_ref, pin_ref,
                       xn_ref, proj_ref, pooled_ref, *, n):
    R = 32
    g_mix = nmix_ref[...]
    for r0 in range(0, n, R):
        xn_ref[r0:r0 + R, :] = _rms_rows(x_ref[r0:r0 + R, :], g_mix).astype(BF16)
    for c0 in range(0, D_PROJ, D):
        proj_ref[:, c0:c0 + D] = _dot(xn_ref[...], w_in_ref[:, c0:c0 + D])

    def gate(i, rows, cols=slice(0, D)):
        c = slice(C_GATE + i * D + cols.start, C_GATE + i * D + cols.stop)
        bc = slice(i * D + cols.start, i * D + cols.stop)
        return jax.nn.sigmoid(proj_ref[rows, c] + gb_ref[:, bc])

    cnt = [float(min(PAST_LEN + 1, w)) for w in POOL_WINDOWS]
    for r0 in range(0, n, R):
        rows = slice(r0, r0 + R)
        u = proj_ref[rows, C_CA:C_GLU_V] * proj_ref[rows, C_HA:C_BA]
        u_ref[rows, :] = u
        y_a = proj_ref[rows, C_BA:C_CA] * (za_ref[0, rows, :] + u * caw_ref[KA - 1:KA, :])
        merged = gate(0, rows) * y_a
        glu = proj_ref[rows, C_GLU_V:C_GLU_G] * jax.nn.sigmoid(proj_ref[rows, C_GLU_G:C_P])
        glu_ref[rows, :] = glu
        z = zb_ref[0, rows, :] + glu * cbw_ref[KB - 1:KB, :] + cbb_ref[...]
        mu = jnp.mean(z, axis=-1, keepdims=True)
        zc = z - mu
        var = jnp.mean(zc * zc, axis=-1, keepdims=True)
        y = zc * lax.rsqrt(var + EPS) * lng_ref[...] + lnb_ref[...]
        y = y * jax.nn.sigmoid(y)
        merged = merged + gate(1, rows) * y
        part_ref[rows, :] = merged
        p_in = proj_ref[rows, C_P:C_Q]
        pin_ref[rows, :] = p_in
        s = sp_ref[0, rows, :] + p_in
        for g in range(N_GROUPS):
            cols = slice(g * GROUP, (g + 1) * GROUP)
            pooled_ref[rows, cols] = (s[:, cols] * (1.0 / cnt[g]) - p_in[:, cols]).astype(BF16)
        q_ref[rows, :] = proj_ref[rows, C_Q:C_GATE]
        g3_ref[rows, :] = gate(3, rows)
    for g in range(N_GROUPS):
        cols = slice(g * GROUP, (g + 1) * GROUP)
        y_c = _dot(pooled_ref[:, cols], pw_ref[g]) * psc_ref[:, cols]
        for r0 in range(0, n, R):
            rows = slice(r0, r0 + R)
            part_ref[rows, cols] += gate(2, rows, cols) * y_c[r0:r0 + R]


def _mix_sample_call(l, x, za, zb, sp, w_in_b, pw_b, nmix, caw, cbw, cbb, lng, lnb, psc, gb):
    n = x.shape[0]
    vec = lambda a: a.reshape(1, -1)
    const = lambda shape: pl.BlockSpec(shape, lambda i: (0,) * len(shape), pipeline_mode=pl.Buffered(1))
    lay = pl.BlockSpec((1, n, D), lambda i: (l, 0, 0))
    row = pl.BlockSpec((n, D), lambda i: (0, 0))
    out = jax.ShapeDtypeStruct((n, D), F32)
    return pl.pallas_call(
        functools.partial(_mix_sample_kernel, n=n),
        grid=(1,),
        in_specs=[row, lay, lay, lay,
                  const((D, D_PROJ)), const((N_GROUPS, GROUP, GROUP)),
                  const((1, D)), const((KA, D)), const((KB, D)),
                  const((1, D)), const((1, D)), const((1, D)), const((1, D)), const((1, 4 * D))],
        out_specs=[row] * 6,
        out_shape=[out] * 6,
        scratch_shapes=[pltpu.VMEM((n, D), BF16), pltpu.VMEM((n, D_PROJ), F32),
                        pltpu.VMEM((n, D), BF16)],
        compiler_params=pltpu.CompilerParams(
            dimension_semantics=("arbitrary",), vmem_limit_bytes=VMEM_LIMIT),
        name=f"mix_sample_l{l}",
    )(x, za, zb, sp, w_in_b, pw_b, vec(nmix), caw, cbw, vec(cbb), vec(lng), vec(lnb), vec(psc), vec(gb))


def _attn_sample_kernel(q_ref, k_ref, v_ref, o_ref, *, bt):
    for i in range(bt):
        q = q_ref[i]
        s = jnp.sum(k_ref[0, i] * q[None], axis=-1, keepdims=True) * (HEAD ** -0.5)
        m = jnp.max(s, axis=0, keepdims=True)
        e = jnp.exp(s - m)
        p = e * (1.0 / jnp.sum(e, axis=0, keepdims=True))
        o_ref[i] = jnp.sum(p * v_ref[0, i], axis=0)


def _attn_sample_call(l, q, cache_k, cache_v):
    n = q.shape[0]
    bt = 4
    kv = pl.BlockSpec((1, bt, N_MEM, N_HEADS, HEAD), lambda i: (l, i, 0, 0, 0))
    qo = pl.BlockSpec((bt, N_HEADS, HEAD), lambda i: (i, 0, 0))
    return pl.pallas_call(
        functools.partial(_attn_sample_kernel, bt=bt),
        grid=(n // bt,),
        in_specs=[qo, kv, kv],
        out_specs=qo,
        out_shape=jax.ShapeDtypeStruct((n, N_HEADS, HEAD), F32),
        compiler_params=pltpu.CompilerParams(
            dimension_semantics=("arbitrary",), vmem_limit_bytes=VMEM_LIMIT),
        name=f"attn_sample_l{l}",
    )(q.reshape(n, N_HEADS, HEAD), cache_k, cache_v).reshape(n, D)


def kernel(x_prompt, x_sample, mem_prompt, cache_mem_k, cache_mem_v, state_conv_a, state_conv_b,
           state_pool, norm_mix, norm_mem, w_kv, w_in, conv_a_w, conv_b_w, conv_b_bias, ln_b_gain,
           ln_b_bias, pool_w, pool_scale, gate_bias, w_o, norm_ffn, w_ff1, w_ff2, norm_final):
    nb, seq, _ = x_prompt.shape
    ns = x_sample.shape[0]

    w_kv_b = w_kv.astype(BF16)
    w_in_b = w_in.astype(BF16)
    w_o_b = w_o.astype(BF16)
    pw_b = pool_w.astype(BF16)
    w1_b = w_ff1.astype(BF16)
    w2_b = w_ff2.astype(BF16)

    mem_k, mem_v, kb, vb = _kv_call(mem_prompt, norm_mem, w_kv_b)
    x = x_prompt
    tails_a, tails_b, tails_p = [], [], []
    for l in range(DEPTH):
        x, ta, tb, tp = _mix_call(l, x, kb, vb, w_in_b[l], w_o_b[l], pw_b[l], norm_mix[l],
                                  conv_a_w[l], conv_b_w[l], conv_b_bias[l], ln_b_gain[l],
                                  ln_b_bias[l], pool_scale[l], gate_bias[l])
        x = _ffn_call(x.reshape(nb * seq, D), norm_ffn[l], w1_b[l], w2_b[l], norm_final,
                      final_norm=(l == DEPTH - 1), name=f"ffn_prompt_l{l}").reshape(nb, seq, D)
        tails_a.append(ta[:, HALO_A - (KA - 1):])
        tails_b.append(tb[:, HALO_B - (KB - 1):])
        tails_p.append(tp[:, HALO_P - POOL_BUF:])
    y_prompt = x
    shape_kv = (DEPTH, nb, N_MEM, N_HEADS, HEAD)
    mem_k_prompt = mem_k.reshape(shape_kv)
    mem_v_prompt = mem_v.reshape(shape_kv)
    conv_a_prompt = jnp.stack(tails_a)
    conv_b_prompt = jnp.stack(tails_b)
    pool_prompt = jnp.stack(tails_p)

    pool_mask = jnp.concatenate(
        [jnp.broadcast_to((jnp.arange(POOL_BUF) >= POOL_BUF + 1 - w).astype(F32)[:, None], (POOL_BUF, GROUP))
         for w in POOL_WINDOWS], axis=1)
    za = _state_reduce_call(state_conv_a, conv_a_w[:, :KA - 1], "state_conv_a")
    zb = _state_reduce_call(state_conv_b, conv_b_w[:, :KB - 1], "state_conv_b")
    sp = _state_reduce_call(state_pool, jnp.broadcast_to(pool_mask, (DEPTH, POOL_BUF, D)), "state_pool")
    xs = x_sample.reshape(ns, D)
    new_a, new_b, new_p = [], [], []
    for l in range(DEPTH):
        part, q, g3, u, glu, p_in = _mix_sample_call(
            l, xs, za, zb, sp, w_in_b[l], pw_b[l], norm_mix[l], conv_a_w[l], conv_b_w[l],
            conv_b_bias[l], ln_b_gain[l], ln_b_bias[l], pool_scale[l], gate_bias[l])
        ym = _attn_sample_call(l, q, cache_mem_k, cache_mem_v)
        xs = _ffn_call(xs, norm_ffn[l], w1_b[l], w2_b[l], norm_final, final_norm=(l == DEPTH - 1),
                       merge=(part, g3, ym, w_o_b[l]), name=f"ffn_sample_l{l}")
        new_a.append(jnp.concatenate([state_conv_a[l, :, 1:], u[:, None]], axis=1))
        new_b.append(jnp.concatenate([state_conv_b[l, :, 1:], glu[:, None]], axis=1))
        new_p.append(jnp.concatenate([state_pool[l, :, 1:], p_in[:, None]], axis=1))
    y_sample = xs.reshape(ns, 1, D)
    conv_a_sample = jnp.stack(new_a)
    conv_b_sample = jnp.stack(new_b)
    pool_sample = jnp.stack(new_p)

    return (y_prompt, y_sample, mem_k_prompt, mem_v_prompt, conv_a_prompt, conv_b_prompt,
            pool_prompt, conv_a_sample, conv_b_sample, pool_sample)
```

```python
import functools

import jax
import jax.numpy as jnp
from jax import lax
from jax.experimental import pallas as pl
from jax.experimental.pallas import tpu as pltpu

D = 1024
DEPTH = 2
N_MEM = 256
N_HEADS = 4
HEAD = D // N_HEADS
N_GROUPS = 4
GROUP = D // N_GROUPS
POOL_WINDOWS = (2, 4, 8, 16)
POOL_BUF = 15
KA = 3
KB = 31
D_FF = 4 * D
D_PROJ = 11 * D
PAST_LEN = 16384
EPS = 1e-6

C_HA, C_BA, C_CA, C_GLU_V, C_GLU_G, C_P, C_Q, C_GATE = (
    0, D, 2 * D, 3 * D, 4 * D, 5 * D, 6 * D, 7 * D)

SUBLANES = 8
LANES = 128
VMEM_LIMIT = 60 * 1024 * 1024

T_MIX = 256
T_FFN = 512
HALO_A, HALO_B, HALO_P = 8, 32, 16
STAGE_ROWS, STAGE_COLS = 512, 1024

BF16 = jnp.bfloat16
F32 = jnp.float32


def _dot(a, b):
    return jnp.dot(a, b, preferred_element_type=F32)


def _rms_rows(x, g):
    ms = jnp.mean(x * x, axis=-1, keepdims=True)
    return x * lax.rsqrt(ms + EPS) * g


def _stage_weights(jobs, stage_ref, sem_ref):
    chunks = [(src, dst, r0, c0) for src, dst in jobs
              for r0 in range(0, src.shape[0], STAGE_ROWS)
              for c0 in range(0, src.shape[1], STAGE_COLS)]

    def copy(i):
        src, _, r0, c0 = chunks[i]
        return pltpu.make_async_copy(src.at[pl.ds(r0, STAGE_ROWS), pl.ds(c0, STAGE_COLS)],
                                     stage_ref.at[i % 2], sem_ref.at[i % 2])

    copy(0).start()
    for i, (_, dst, r0, c0) in enumerate(chunks):
        if i + 1 < len(chunks):
            copy(i + 1).start()
        copy(i).wait()
        dst[r0:r0 + STAGE_ROWS, c0:c0 + STAGE_COLS] = stage_ref[i % 2].astype(BF16)


def _stage_scratch():
    return [pltpu.VMEM((2, STAGE_ROWS, STAGE_COLS), F32), pltpu.SemaphoreType.DMA((2,))]


def _shift_rows(v, s, halo, rows):
    return v[halo - s:halo - s + rows]


def _kv_kernel(mem_ref, g_ref, w_ref, k_ref, v_ref, kb_ref, vb_ref):
    memn = _rms_rows(mem_ref[0], g_ref[0]).astype(BF16)
    kv = _dot(memn, w_ref[0])
    k = kv[:, :D]
    v = kv[:, D:]
    k_ref[0, 0] = k
    v_ref[0, 0] = v
    kb_ref[0, 0] = k.astype(BF16)
    vb_ref[0, 0] = v.astype(BF16)


def _kv_call(mem, norm_mem, w_kv_b):
    nb = mem.shape[0]
    out_f = jax.ShapeDtypeStruct((DEPTH, nb, N_MEM, D), F32)
    out_b = jax.ShapeDtypeStruct((DEPTH, nb, N_MEM, D), BF16)
    blk = pl.BlockSpec((1, 1, N_MEM, D), lambda l, b: (l, b, 0, 0))
    return pl.pallas_call(
        _kv_kernel,
        grid=(DEPTH, nb),
        in_specs=[pl.BlockSpec((1, N_MEM, D), lambda l, b: (b, 0, 0)),
                  pl.BlockSpec((1, 1, D), lambda l, b: (l, 0, 0)),
                  pl.BlockSpec((1, D, 2 * D), lambda l, b: (l, 0, 0))],
        out_specs=[blk, blk, blk, blk],
        out_shape=[out_f, out_f, out_b, out_b],
        compiler_params=pltpu.CompilerParams(
            dimension_semantics=("arbitrary", "arbitrary"), vmem_limit_bytes=VMEM_LIMIT),
        name="kv_proj",
    )(mem, norm_mem.reshape(DEPTH, 1, D), w_kv_b)


def _mix_kernel(x_ref, kb_ref, vb_ref, w_in_hbm, w_o_hbm, pw_ref, nmix_ref, caw_ref,
                cbw_ref, cbb_ref, lng_ref, lnb_ref, psc_ref, gb_ref,
                x1_ref, ta_ref, tb_ref, tp_ref,
                w_in_ref, w_o_ref, stage_ref, stage_sem,
                xn_ref, pb_ref, proj_ref, ext_a, ext_b, ext_p, z_ref, mg_ref, mgb_ref, pooled_ref,
                p_ref, *, l):
    T = T_MIX
    j = pl.program_id(1)

    @pl.when((pl.program_id(0) == 0) & (j == 0))
    def _():
        _stage_weights([(w_in_hbm.at[l], w_in_ref), (w_o_hbm.at[l], w_o_ref)], stage_ref, stage_sem)

    @pl.when(j == 0)
    def _():
        ext_a[0:HALO_A, :] = jnp.zeros((HALO_A, D), F32)
        ext_b[0:HALO_B, :] = jnp.zeros((HALO_B, D), F32)
        ext_p[0:HALO_P, :] = jnp.zeros((HALO_P, D), F32)

    g_mix = nmix_ref[...]
    for r0 in range(0, T, 32):
        xn_ref[r0:r0 + 32, :] = _rms_rows(x_ref[0, r0:r0 + 32, :], g_mix).astype(BF16)

    pb_ref[...] = _dot(xn_ref[...], w_in_ref[:, C_GLU_V:C_P])
    P_Q, P_GATE = 3 * D, 4 * D
    proj_ref[:, 0:P_Q] = _dot(xn_ref[...], w_in_ref[:, C_HA:C_GLU_V])
    ext_p[HALO_P:HALO_P + T, :] = _dot(xn_ref[...], w_in_ref[:, C_P:C_Q])
    proj_ref[:, P_Q:P_GATE] = _dot(xn_ref[...], w_in_ref[:, C_Q:C_GATE])
    proj_ref[:, P_GATE:P_GATE + 4 * D] = _dot(xn_ref[...], w_in_ref[:, C_GATE:D_PROJ])

    def gate(i, rows, cols=slice(0, D)):
        c = slice(P_GATE + i * D + cols.start, P_GATE + i * D + cols.stop)
        bc = slice(i * D + cols.start, i * D + cols.stop)
        return jax.nn.sigmoid(proj_ref[rows, c] + gb_ref[:, bc])

    for r0 in range(0, T, 32):
        rows = slice(r0, r0 + 32)
        glu = pb_ref[rows, 0:D] * jax.nn.sigmoid(pb_ref[rows, D:2 * D])
        ext_b[HALO_B + r0:HALO_B + r0 + 32, :] = glu

    R = 128
    for r0 in range(0, T, R):
        for c0 in range(0, D, LANES):
            cols = slice(c0, c0 + LANES)
            acc = None
            for b in range(SUBLANES):
                q = None
                for a in range(4):
                    d = SUBLANES * a + b
                    if d > KB - 1:
                        continue
                    base = HALO_B + r0 - SUBLANES - SUBLANES * a
                    term = ext_b[base:base + R + SUBLANES, cols] * cbw_ref[KB - 1 - d:KB - d, cols]
                    q = term if q is None else q + term
                sh = _shift_rows(q, b, SUBLANES, R)
                acc = sh if acc is None else acc + sh
            z_ref[r0:r0 + R, cols] = acc + cbb_ref[:, cols]
    for r0 in range(0, T, 16):
        rows = slice(r0, r0 + 16)
        z = z_ref[rows, :]
        mu = jnp.mean(z, axis=-1, keepdims=True)
        zc = z - mu
        var = jnp.mean(zc * zc, axis=-1, keepdims=True)
        y = zc * lax.rsqrt(var + EPS) * lng_ref[...] + lnb_ref[...]
        y = y * jax.nn.sigmoid(y)
        mg_ref[rows, :] = gate(1, rows) * y

    for r0 in range(0, T, 32):
        rows = slice(r0, r0 + 32)
        ext_a[HALO_A + r0:HALO_A + r0 + 32, :] = proj_ref[rows, C_CA:C_GLU_V] * proj_ref[rows, C_HA:C_BA]
    for r0 in range(0, T, 32):
        rows = slice(r0, r0 + 32)
        for c0 in range(0, D, 256):
            cols = slice(c0, c0 + 256)
            e = ext_a[r0:r0 + 32 + HALO_A, cols]
            conv = (e[HALO_A:] * caw_ref[2:3, cols]
                    + _shift_rows(e, 1, HALO_A, 32) * caw_ref[1:2, cols]
                    + _shift_rows(e, 2, HALO_A, 32) * caw_ref[0:1, cols])
            y_a = proj_ref[rows, C_BA + c0:C_BA + c0 + 256] * conv
            mg_ref[rows, cols] += gate(0, rows, cols) * y_a

    for r0 in range(0, T, 32):
        pos1 = j * T + r0 + 1 + lax.broadcasted_iota(jnp.int32, (32, 1), 0)
        for g, w in enumerate(POOL_WINDOWS):
            cols = slice(g * GROUP, (g + 1) * GROUP)
            e = ext_p[r0:r0 + 32 + HALO_P, cols]
            s = e
            stride = 1
            while stride < w:
                s = s + _roll_down(s, stride)
                stride *= 2
            cnt = jnp.minimum(pos1, w).astype(F32)
            tok = e[HALO_P:]
            pooled = s[HALO_P:] * (1.0 / cnt) - tok
            pooled_ref[r0:r0 + 32, cols] = pooled.astype(BF16)
    for g in range(N_GROUPS):
        cols = slice(g * GROUP, (g + 1) * GROUP)
        y_c = _dot(pooled_ref[:, cols], pw_ref[g]) * psc_ref[:, cols]
        for r0 in range(0, T, 64):
            rows = slice(r0, r0 + 64)
            mg_ref[rows, cols] += gate(2, rows, cols) * y_c[r0:r0 + 64]

    for h in range(N_HEADS):
        cols = slice(h * HEAD, (h + 1) * HEAD)
        qh = proj_ref[:, P_Q + h * HEAD:P_Q + (h + 1) * HEAD].astype(BF16)
        s = lax.dot_general(qh, kb_ref[0, 0, :, cols], (((1,), (1,)), ((), ())),
                            preferred_element_type=F32) * (HEAD ** -0.5)
        for r0 in range(0, T, 64):
            sc = s[r0:r0 + 64]
            m = jnp.max(sc, axis=-1, keepdims=True)
            e = jnp.exp(sc - m)
            p = e * (1.0 / jnp.sum(e, axis=-1, keepdims=True))
            p_ref[r0:r0 + 64, :] = p.astype(BF16)
        y_m = _dot(p_ref[...], vb_ref[0, 0, :, cols])
        for r0 in range(0, T, 64):
            rows = slice(r0, r0 + 64)
            mg_ref[rows, cols] += gate(3, rows, cols) * y_m[r0:r0 + 64]

    for r0 in range(0, T, 32):
        mgb_ref[r0:r0 + 32, :] = mg_ref[r0:r0 + 32, :].astype(BF16)
    x1_ref[0] = x_ref[0] + _dot(mgb_ref[...], w_o_ref[...])

    ta = ext_a[T:T + HALO_A, :]
    tb = ext_b[T:T + HALO_B, :]
    tp = ext_p[T:T + HALO_P, :]
    ta_ref[0] = ta
    tb_ref[0] = tb
    tp_ref[0] = tp
    ext_a[0:HALO_A, :] = ta
    ext_b[0:HALO_B, :] = tb
    ext_p[0:HALO_P, :] = tp


def _roll_down(v, s):
    return pltpu.roll(v, s, axis=0)


def _const_spec(shape):
    nd = len(shape)
    return pl.BlockSpec(shape, lambda b, j: (0,) * nd, pipeline_mode=pl.Buffered(1))


def _mix_call(l, x, kb, vb, w_in, w_o, pw_b, nmix, caw, cbw, cbb, lng, lnb, psc, gb):
    nb, seq, _ = x.shape
    T = T_MIX
    vec = lambda a: a.reshape(1, -1)
    kv_spec = pl.BlockSpec((1, 1, N_MEM, D), lambda b, j: (l, b, 0, 0))
    hbm = pl.BlockSpec(memory_space=pl.ANY)
    return pl.pallas_call(
        functools.partial(_mix_kernel, l=l),
        grid=(nb, seq // T),
        in_specs=[pl.BlockSpec((1, T, D), lambda b, j: (b, j, 0)),
                  kv_spec, kv_spec,
                  hbm, hbm, _const_spec((N_GROUPS, GROUP, GROUP)),
                  _const_spec((1, D)), _const_spec((KA, D)), _const_spec((KB, D)),
                  _const_spec((1, D)), _const_spec((1, D)), _const_spec((1, D)), _const_spec((1, D)),
                  _const_spec((1, 4 * D))],
        out_specs=[pl.BlockSpec((1, T, D), lambda b, j: (b, j, 0)),
                   pl.BlockSpec((1, HALO_A, D), lambda b, j: (b, 0, 0)),
                   pl.BlockSpec((1, HALO_B, D), lambda b, j: (b, 0, 0)),
                   pl.BlockSpec((1, HALO_P, D), lambda b, j: (b, 0, 0))],
        out_shape=[jax.ShapeDtypeStruct((nb, seq, D), F32),
                   jax.ShapeDtypeStruct((nb, HALO_A, D), F32),
                   jax.ShapeDtypeStruct((nb, HALO_B, D), F32),
                   jax.ShapeDtypeStruct((nb, HALO_P, D), F32)],
        scratch_shapes=[pltpu.VMEM((D, D_PROJ), BF16),
                        pltpu.VMEM((D, D), BF16)]
        + _stage_scratch()
        + [pltpu.VMEM((T, D), BF16),
           pltpu.VMEM((T, 2 * D), F32),
           pltpu.VMEM((T, 8 * D), F32),
           pltpu.VMEM((HALO_A + T, D), F32),
           pltpu.VMEM((HALO_B + T, D), F32),
           pltpu.VMEM((HALO_P + T, D), F32),
           pltpu.VMEM((T, D), F32),
           pltpu.VMEM((T, D), F32),
           pltpu.VMEM((T, D), BF16),
           pltpu.VMEM((T, D), BF16),
           pltpu.VMEM((T, N_MEM), BF16)],
        compiler_params=pltpu.CompilerParams(
            dimension_semantics=("arbitrary", "arbitrary"), vmem_limit_bytes=VMEM_LIMIT),
        name=f"mix_prompt_l{l}",
    )(x, kb, vb, w_in, w_o, pw_b, vec(nmix), caw, cbw, vec(cbb), vec(lng), vec(lnb), vec(psc),
      vec(gb))


def _ffn_kernel(*refs, l, tile, with_merge, final_norm):
    R = 32
    if with_merge:
        (x_ref, part_ref, g3_ref, ym_ref, w_o_hbm, nffn_ref, w1_hbm, w2_hbm, nfin_ref,
         o_ref, w1_ref, w2_ref, stage_ref, stage_sem, xn_ref, h_ref, w_o_ref, x1_ref) = refs
        jobs = [(w_o_hbm.at[l], w_o_ref)]
    else:
        (x1_ref, nffn_ref, w1_hbm, w2_hbm, nfin_ref,
         o_ref, w1_ref, w2_ref, stage_ref, stage_sem, xn_ref, h_ref) = refs
        jobs = []
    jobs += [(w1_hbm.at[l], w1_ref), (w2_hbm.at[l], w2_ref)]

    @pl.when(pl.program_id(0) == 0)
    def _():
        _stage_weights(jobs, stage_ref, stage_sem)

    if with_merge:
        for r0 in range(0, tile, R):
            rows = slice(r0, r0 + R)
            xn_ref[rows, :] = (part_ref[rows, :] + g3_ref[rows, :] * ym_ref[rows, :]).astype(BF16)
        x1_ref[...] = x_ref[...] + _dot(xn_ref[...], w_o_ref[...])
    g_ffn = nffn_ref[...]
    for r0 in range(0, tile, R):
        rows = slice(r0, r0 + R)
        xn_ref[rows, :] = _rms_rows(x1_ref[rows, :], g_ffn).astype(BF16)
    for f0 in range(0, D_FF, D):
        h = jnp.maximum(_dot(xn_ref[...], w1_ref[:, f0:f0 + D]), 0.0)
        h_ref[:, f0:f0 + D] = (h * h).astype(BF16)
    y = x1_ref[...] + _dot(h_ref[...], w2_ref[...])
    if final_norm:
        y = _rms_rows(y, nfin_ref[...])
    o_ref[...] = y


def _ffn_call(l, x, nffn, w1, w2, nfin, *, final_norm, merge=None, name):
    n = x.shape[0]
    tile = min(T_FFN, n)
    row = pl.BlockSpec((tile, D), lambda i: (i, 0))
    const = lambda shape: pl.BlockSpec(shape, lambda i: (0,) * len(shape), pipeline_mode=pl.Buffered(1))
    hbm = pl.BlockSpec(memory_space=pl.ANY)
    args, specs = [x], [row]
    if merge is not None:
        part, g3, ym, w_o = merge
        args += [part, g3, ym, w_o]
        specs += [row, row, row, hbm]
    args += [nffn.reshape(1, D), w1, w2, nfin.reshape(1, D)]
    specs += [const((1, D)), hbm, hbm, const((1, D))]
    return pl.pallas_call(
        functools.partial(_ffn_kernel, l=l, tile=tile, with_merge=merge is not None,
                          final_norm=final_norm),
        grid=(n // tile,),
        in_specs=specs,
        out_specs=row,
        out_shape=jax.ShapeDtypeStruct((n, D), F32),
        scratch_shapes=[pltpu.VMEM((D, D_FF), BF16), pltpu.VMEM((D_FF, D), BF16)]
        + _stage_scratch()
        + [pltpu.VMEM((tile, D), BF16), pltpu.VMEM((tile, D_FF), BF16)]
        + ([pltpu.VMEM((D, D), BF16), pltpu.VMEM((tile, D), F32)] if merge is not None else []),
        compiler_params=pltpu.CompilerParams(
            dimension_semantics=("arbitrary",), vmem_limit_bytes=VMEM_LIMIT),
        name=name,
    )(*args)


def _state_kernel(st_ref, wt_ref, red_ref, sh_ref, *, k_len):
    k = pl.program_id(1)
    st = st_ref[0, 0]
    sh_ref[0, 0] = st

    @pl.when(k == 0)
    def _():
        red_ref[0] = jnp.zeros_like(st)

    @pl.when(k < k_len)
    def _():
        red_ref[0] += st * wt_ref[0, pl.ds(k, 1), :]


def _state_call(state_t, wt, name):
    _, k_len, n, _ = state_t.shape
    return pl.pallas_call(
        functools.partial(_state_kernel, k_len=k_len),
        grid=(DEPTH, k_len + 1),
        in_specs=[pl.BlockSpec((1, 1, n, D), lambda l, k: (l, jnp.minimum(k, k_len - 1), 0, 0)),
                  pl.BlockSpec((1, k_len, D), lambda l, k: (l, 0, 0))],
        out_specs=[pl.BlockSpec((1, n, D), lambda l, k: (l, 0, 0)),
                   pl.BlockSpec((1, 1, n, D), lambda l, k: (l, jnp.maximum(k - 1, 0), 0, 0))],
        out_shape=[jax.ShapeDtypeStruct((DEPTH, n, D), F32),
                   jax.ShapeDtypeStruct(state_t.shape, F32)],
        compiler_params=pltpu.CompilerParams(
            dimension_semantics=("arbitrary", "arbitrary"), vmem_limit_bytes=VMEM_LIMIT),
        name=name,
    )(state_t, wt)


def _mix_sample_kernel(x_ref, za_ref, zb_ref, sp_ref, w_in_hbm, pw_ref, nmix_ref, caw_ref, cbw_ref,
                       cbb_ref, lng_ref, lnb_ref, psc_ref, gb_ref, sha_in, shb_in, shp_in,
                       part_ref, q_ref, g3_ref, u_ref, glu_ref, pin_ref,
                       w_in_ref, stage_ref, stage_sem, xn_ref, proj_ref, pooled_ref, *, l, n):
    del sha_in, shb_in, shp_in
    _stage_weights([(w_in_hbm.at[l], w_in_ref)], stage_ref, stage_sem)
    R = 32
    g_mix = nmix_ref[...]
    for r0 in range(0, n, R):
        xn_ref[r0:r0 + R, :] = _rms_rows(x_ref[r0:r0 + R, :], g_mix).astype(BF16)
    for c0 in range(0, D_PROJ, D):
        proj_ref[:, c0:c0 + D] = _dot(xn_ref[...], w_in_ref[:, c0:c0 + D])

    def gate(i, rows, cols=slice(0, D)):
        c = slice(C_GATE + i * D + cols.start, C_GATE + i * D + cols.stop)
        bc = slice(i * D + cols.start, i * D + cols.stop)
        return jax.nn.sigmoid(proj_ref[rows, c] + gb_ref[:, bc])

    cnt = [float(min(PAST_LEN + 1, w)) for w in POOL_WINDOWS]
    for r0 in range(0, n, R):
        rows = slice(r0, r0 + R)
        u = proj_ref[rows, C_CA:C_GLU_V] * proj_ref[rows, C_HA:C_BA]
        u_ref[0, 0, rows, :] = u
        y_a = proj_ref[rows, C_BA:C_CA] * (za_ref[0, rows, :] + u * caw_ref[KA - 1:KA, :])
        merged = gate(0, rows) * y_a
        glu = proj_ref[rows, C_GLU_V:C_GLU_G] * jax.nn.sigmoid(proj_ref[rows, C_GLU_G:C_P])
        glu_ref[0, 0, rows, :] = glu
        z = zb_ref[0, rows, :] + glu * cbw_ref[KB - 1:KB, :] + cbb_ref[...]
        mu = jnp.mean(z, axis=-1, keepdims=True)
        zc = z - mu
        var = jnp.mean(zc * zc, axis=-1, keepdims=True)
        y = zc * lax.rsqrt(var + EPS) * lng_ref[...] + lnb_ref[...]
        y = y * jax.nn.sigmoid(y)
        merged = merged + gate(1, rows) * y
        part_ref[rows, :] = merged
        p_in = proj_ref[rows, C_P:C_Q]
        pin_ref[0, 0, rows, :] = p_in
        s = sp_ref[0, rows, :] + p_in
        for g in range(N_GROUPS):
            cols = slice(g * GROUP, (g + 1) * GROUP)
            pooled_ref[rows, cols] = (s[:, cols] * (1.0 / cnt[g]) - p_in[:, cols]).astype(BF16)
        q_ref[rows, :] = proj_ref[rows, C_Q:C_GATE]
        g3_ref[rows, :] = gate(3, rows)
    for g in range(N_GROUPS):
        cols = slice(g * GROUP, (g + 1) * GROUP)
        y_c = _dot(pooled_ref[:, cols], pw_ref[g]) * psc_ref[:, cols]
        for r0 in range(0, n, R):
            rows = slice(r0, r0 + R)
            part_ref[rows, cols] += gate(2, rows, cols) * y_c[r0:r0 + R]


def _mix_sample_call(l, x, za, zb, sp, w_in, pw_b, nmix, caw, cbw, cbb, lng, lnb, psc, gb,
                     sha, shb, shp):
    n = x.shape[0]
    vec = lambda a: a.reshape(1, -1)
    const = lambda shape: pl.BlockSpec(shape, lambda i: (0,) * len(shape), pipeline_mode=pl.Buffered(1))
    lay = pl.BlockSpec((1, n, D), lambda i: (l, 0, 0))
    row = pl.BlockSpec((n, D), lambda i: (0, 0))
    out = jax.ShapeDtypeStruct((n, D), F32)
    states = (sha, shb, shp)
    last_slot = lambda st: pl.BlockSpec((1, 1, n, D), lambda i: (l, st.shape[1] - 1, 0, 0))
    n_in = 14
    return pl.pallas_call(
        functools.partial(_mix_sample_kernel, l=l, n=n),
        grid=(1,),
        in_specs=[row, lay, lay, lay,
                  pl.BlockSpec(memory_space=pl.ANY), const((N_GROUPS, GROUP, GROUP)),
                  const((1, D)), const((KA, D)), const((KB, D)),
                  const((1, D)), const((1, D)), const((1, D)), const((1, D)), const((1, 4 * D))]
        + [pl.BlockSpec(memory_space=pl.ANY)] * 3,
        out_specs=[row] * 3 + [last_slot(st) for st in states],
        out_shape=[out] * 3 + [jax.ShapeDtypeStruct(st.shape, F32) for st in states],
        input_output_aliases={n_in: 3, n_in + 1: 4, n_in + 2: 5},
        scratch_shapes=[pltpu.VMEM((D, D_PROJ), BF16)] + _stage_scratch()
        + [pltpu.VMEM((n, D), BF16), pltpu.VMEM((n, D_PROJ), F32), pltpu.VMEM((n, D), BF16)],
        compiler_params=pltpu.CompilerParams(
            dimension_semantics=("arbitrary",), vmem_limit_bytes=VMEM_LIMIT),
        name=f"mix_sample_l{l}",
    )(x, za, zb, sp, w_in, pw_b, vec(nmix), caw, cbw, vec(cbb), vec(lng), vec(lnb), vec(psc), vec(gb),
      sha, shb, shp)


def _attn_sample_kernel(q_ref, k_ref, v_ref, o_ref, *, bt):
    for i in range(bt):
        q = q_ref[i]
        s = jnp.sum(k_ref[0, i] * q[None], axis=-1, keepdims=True) * (HEAD ** -0.5)
        m = jnp.max(s, axis=0, keepdims=True)
        e = jnp.exp(s - m)
        p = e * (1.0 / jnp.sum(e, axis=0, keepdims=True))
        o_ref[i] = jnp.sum(p * v_ref[0, i], axis=0)


def _attn_sample_call(l, q, cache_k, cache_v):
    n = q.shape[0]
    bt = 4
    kv = pl.BlockSpec((1, bt, N_MEM, N_HEADS, HEAD), lambda i: (l, i, 0, 0, 0))
    qo = pl.BlockSpec((bt, N_HEADS, HEAD), lambda i: (i, 0, 0))
    return pl.pallas_call(
        functools.partial(_attn_sample_kernel, bt=bt),
        grid=(n // bt,),
        in_specs=[qo, kv, kv],
        out_specs=qo,
        out_shape=jax.ShapeDtypeStruct((n, N_HEADS, HEAD), F32),
        compiler_params=pltpu.CompilerParams(
            dimension_semantics=("arbitrary",), vmem_limit_bytes=VMEM_LIMIT),
        name=f"attn_sample_l{l}",
    )(q.reshape(n, N_HEADS, HEAD), cache_k, cache_v).reshape(n, D)


def kernel(x_prompt, x_sample, mem_prompt, cache_mem_k, cache_mem_v, state_conv_a, state_conv_b,
           state_pool, norm_mix, norm_mem, w_kv, w_in, conv_a_w, conv_b_w, conv_b_bias, ln_b_gain,
           ln_b_bias, pool_w, pool_scale, gate_bias, w_o, norm_ffn, w_ff1, w_ff2, norm_final):
    nb, seq, _ = x_prompt.shape
    ns = x_sample.shape[0]

    w_kv_b = w_kv.astype(BF16)
    pw_b = pool_w.astype(BF16)

    mem_k, mem_v, kb, vb = _kv_call(mem_prompt, norm_mem, w_kv_b)
    x = x_prompt
    tails_a, tails_b, tails_p = [], [], []
    for l in range(DEPTH):
        x, ta, tb, tp = _mix_call(l, x, kb, vb, w_in, w_o, pw_b[l], norm_mix[l],
                                  conv_a_w[l], conv_b_w[l], conv_b_bias[l], ln_b_gain[l],
                                  ln_b_bias[l], pool_scale[l], gate_bias[l])
        x = _ffn_call(l, x.reshape(nb * seq, D), norm_ffn[l], w_ff1, w_ff2, norm_final,
                      final_norm=(l == DEPTH - 1), name=f"ffn_prompt_l{l}").reshape(nb, seq, D)
        tails_a.append(ta[:, HALO_A - (KA - 1):])
        tails_b.append(tb[:, HALO_B - (KB - 1):])
        tails_p.append(tp[:, HALO_P - POOL_BUF:])
    y_prompt = x
    shape_kv = (DEPTH, nb, N_MEM, N_HEADS, HEAD)
    mem_k_prompt = mem_k.reshape(shape_kv)
    mem_v_prompt = mem_v.reshape(shape_kv)
    conv_a_prompt = jnp.stack(tails_a)
    conv_b_prompt = jnp.stack(tails_b)
    pool_prompt = jnp.stack(tails_p)

    pool_mask = jnp.concatenate(
        [jnp.broadcast_to((jnp.arange(POOL_BUF) >= POOL_BUF + 1 - w).astype(F32)[:, None], (POOL_BUF, GROUP))
         for w in POOL_WINDOWS], axis=1)
    time_major = lambda a: jnp.transpose(a, (0, 2, 1, 3))
    za, sha = _state_call(time_major(state_conv_a), conv_a_w[:, :KA - 1], "state_conv_a")
    zb, shb = _state_call(time_major(state_conv_b), conv_b_w[:, :KB - 1], "state_conv_b")
    sp, shp = _state_call(time_major(state_pool), jnp.broadcast_to(pool_mask, (DEPTH, POOL_BUF, D)),
                          "state_pool")
    xs = x_sample.reshape(ns, D)
    for l in range(DEPTH):
        part, q, g3, sha, shb, shp = _mix_sample_call(
            l, xs, za, zb, sp, w_in, pw_b[l], norm_mix[l], conv_a_w[l], conv_b_w[l],
            conv_b_bias[l], ln_b_gain[l], ln_b_bias[l], pool_scale[l], gate_bias[l], sha, shb, shp)
        ym = _attn_sample_call(l, q, cache_mem_k, cache_mem_v)
        xs = _ffn_call(l, xs, norm_ffn[l], w_ff1, w_ff2, norm_final, final_norm=(l == DEPTH - 1),
                       merge=(part, g3, ym, w_o), name=f"ffn_sample_l{l}")
    y_sample = xs.reshape(ns, 1, D)
    conv_a_sample = time_major(sha)
    conv_b_sample = time_major(shb)
    pool_sample = time_major(shp)

    return (y_prompt, y_sample, mem_k_prompt, mem_v_prompt, conv_a_prompt, conv_b_prompt,
            pool_prompt, conv_a_sample, conv_b_sample, pool_sample)
```

```python
import functools

import jax
import jax.numpy as jnp
from jax import lax
from jax.experimental import pallas as pl
from jax.experimental.pallas import tpu as pltpu

D = 1024
DEPTH = 2
N_MEM = 256
N_HEADS = 4
HEAD = D // N_HEADS
N_GROUPS = 4
GROUP = D // N_GROUPS
POOL_WINDOWS = (2, 4, 8, 16)
POOL_BUF = 15
KA = 3
KB = 31
D_FF = 4 * D
D_PROJ = 11 * D
PAST_LEN = 16384
EPS = 1e-6

C_HA, C_BA, C_CA, C_GLU_V, C_GLU_G, C_P, C_Q, C_GATE = (
    0, D, 2 * D, 3 * D, 4 * D, 5 * D, 6 * D, 7 * D)

SUBLANES = 8
LANES = 128
VMEM_LIMIT = 60 * 1024 * 1024

T_MIX = 256
T_FFN = 512
HALO_A, HALO_B, HALO_P = 8, 32, 16
STAGE_ROWS, STAGE_COLS = 256, 1024
STAGE_SLOTS = 4

BF16 = jnp.bfloat16
F32 = jnp.float32


def _dot(a, b):
    return jnp.dot(a, b, preferred_element_type=F32)


def _rms_rows(x, g):
    ms = jnp.mean(x * x, axis=-1, keepdims=True)
    return x * lax.rsqrt(ms + EPS) * g


def _stage_weights(jobs, stage_ref, sem_ref):
    chunks = [(src, dst, r0, c0) for src, dst in jobs
              for r0 in range(0, src.shape[0], STAGE_ROWS)
              for c0 in range(0, src.shape[1], STAGE_COLS)]
    ahead = STAGE_SLOTS - 1

    def copy(i):
        src, _, r0, c0 = chunks[i]
        return pltpu.make_async_copy(src.at[pl.ds(r0, STAGE_ROWS), pl.ds(c0, STAGE_COLS)],
                                     stage_ref.at[i % STAGE_SLOTS], sem_ref.at[i % STAGE_SLOTS])

    for i in range(min(ahead, len(chunks))):
        copy(i).start()
    for i, (_, dst, r0, c0) in enumerate(chunks):
        if i + ahead < len(chunks):
            copy(i + ahead).start()
        copy(i).wait()
        dst[r0:r0 + STAGE_ROWS, c0:c0 + STAGE_COLS] = stage_ref[i % STAGE_SLOTS].astype(BF16)


def _stage_scratch():
    return [pltpu.VMEM((STAGE_SLOTS, STAGE_ROWS, STAGE_COLS), F32),
            pltpu.SemaphoreType.DMA((STAGE_SLOTS,))]


def _shift_rows(v, s, halo, rows):
    return v[halo - s:halo - s + rows]


def _kv_kernel(mem_ref, g_ref, w_ref, k_ref, v_ref, kb_ref, vb_ref):
    memn = _rms_rows(mem_ref[0], g_ref[0]).astype(BF16)
    kv = _dot(memn, w_ref[0])
    k = kv[:, :D]
    v = kv[:, D:]
    k_ref[0, 0] = k.reshape(N_MEM, N_HEADS, HEAD)
    v_ref[0, 0] = v.reshape(N_MEM, N_HEADS, HEAD)
    kb_ref[0, 0] = k.astype(BF16)
    vb_ref[0, 0] = v.astype(BF16)


def _kv_call(mem, norm_mem, w_kv_b):
    nb = mem.shape[0]
    out_f = jax.ShapeDtypeStruct((DEPTH, nb, N_MEM, N_HEADS, HEAD), F32)
    out_b = jax.ShapeDtypeStruct((DEPTH, nb, N_MEM, D), BF16)
    blk = pl.BlockSpec((1, 1, N_MEM, D), lambda l, b: (l, b, 0, 0))
    blk_f = pl.BlockSpec((1, 1, N_MEM, N_HEADS, HEAD), lambda l, b: (l, b, 0, 0, 0))
    return pl.pallas_call(
        _kv_kernel,
        grid=(DEPTH, nb),
        in_specs=[pl.BlockSpec((1, N_MEM, D), lambda l, b: (b, 0, 0)),
                  pl.BlockSpec((1, 1, D), lambda l, b: (l, 0, 0)),
                  pl.BlockSpec((1, D, 2 * D), lambda l, b: (l, 0, 0))],
        out_specs=[blk_f, blk_f, blk, blk],
        out_shape=[out_f, out_f, out_b, out_b],
        compiler_params=pltpu.CompilerParams(
            dimension_semantics=("arbitrary", "arbitrary"), vmem_limit_bytes=VMEM_LIMIT),
        name="kv_proj",
    )(mem, norm_mem.reshape(DEPTH, 1, D), w_kv_b)


def _mix_kernel(x_ref, kb_ref, vb_ref, w_in_hbm, w_o_hbm, pw_ref, nmix_ref, caw_ref,
                cbw_ref, cbb_ref, lng_ref, lnb_ref, psc_ref, gb_ref,
                x1_ref, ta_ref, tb_ref, tp_ref,
                w_in_ref, w_o_ref, stage_ref, stage_sem,
                xn_ref, pb_ref, proj_ref, ext_a, ext_b, ext_p, z_ref, mg_ref, mgb_ref, pooled_ref,
                p_ref, s_ref, *, l):
    T = T_MIX
    j = pl.program_id(1)

    @pl.when((pl.program_id(0) == 0) & (j == 0))
    def _():
        _stage_weights([(w_in_hbm.at[l], w_in_ref), (w_o_hbm.at[l], w_o_ref)], stage_ref, stage_sem)

    @pl.when(j == 0)
    def _():
        ext_a[0:HALO_A, :] = jnp.zeros((HALO_A, D), F32)
        ext_b[0:HALO_B, :] = jnp.zeros((HALO_B, D), F32)
        ext_p[0:HALO_P, :] = jnp.zeros((HALO_P, D), F32)

    g_mix = nmix_ref[...]
    for r0 in range(0, T, 32):
        xn_ref[r0:r0 + 32, :] = _rms_rows(x_ref[0, r0:r0 + 32, :], g_mix).astype(BF16)

    pb_ref[...] = _dot(xn_ref[...], w_in_ref[:, C_GLU_V:C_P])
    P_Q, P_GATE = 3 * D, 4 * D
    proj_ref[:, 0:P_Q] = _dot(xn_ref[...], w_in_ref[:, C_HA:C_GLU_V])
    ext_p[HALO_P:HALO_P + T, :] = _dot(xn_ref[...], w_in_ref[:, C_P:C_Q])
    proj_ref[:, P_Q:P_GATE] = _dot(xn_ref[...], w_in_ref[:, C_Q:C_GATE])

    def project_gate(i):
        proj_ref[:, P_GATE + i * D:P_GATE + (i + 1) * D] = _dot(
            xn_ref[...], w_in_ref[:, C_GATE + i * D:C_GATE + (i + 1) * D])

    for h in range(N_HEADS):
        cols = slice(h * HEAD, (h + 1) * HEAD)
        qh = proj_ref[:, P_Q + h * HEAD:P_Q + (h + 1) * HEAD].astype(BF16)
        s_ref[:, cols] = lax.dot_general(qh, kb_ref[0, 0, :, cols], (((1,), (1,)), ((), ())),
                                         preferred_element_type=F32)
    project_gate(1)
    project_gate(0)
    project_gate(2)

    def gate(i, rows, cols=slice(0, D)):
        c = slice(P_GATE + i * D + cols.start, P_GATE + i * D + cols.stop)
        bc = slice(i * D + cols.start, i * D + cols.stop)
        return jax.nn.sigmoid(proj_ref[rows, c] + gb_ref[:, bc])

    for r0 in range(0, T, 32):
        rows = slice(r0, r0 + 32)
        glu = pb_ref[rows, 0:D] * jax.nn.sigmoid(pb_ref[rows, D:2 * D])
        ext_b[HALO_B + r0:HALO_B + r0 + 32, :] = glu

    R = 128
    for r0 in range(0, T, R):
        for c0 in range(0, D, LANES):
            cols = slice(c0, c0 + LANES)
            acc = None
            for b in range(SUBLANES):
                q = None
                for a in range(4):
                    d = SUBLANES * a + b
                    if d > KB - 1:
                        continue
                    base = HALO_B + r0 - SUBLANES - SUBLANES * a
                    term = ext_b[base:base + R + SUBLANES, cols] * cbw_ref[KB - 1 - d:KB - d, cols]
                    q = term if q is None else q + term
                sh = _shift_rows(q, b, SUBLANES, R)
                acc = sh if acc is None else acc + sh
            z_ref[r0:r0 + R, cols] = acc + cbb_ref[:, cols]

    for r0 in range(0, T, 32):
        pos1 = j * T + r0 + 1 + lax.broadcasted_iota(jnp.int32, (32, 1), 0)
        for g, w in enumerate(POOL_WINDOWS):
            cols = slice(g * GROUP, (g + 1) * GROUP)
            e = ext_p[r0:r0 + 32 + HALO_P, cols]
            s = e
            stride = 1
            while stride < w:
                s = s + _roll_down(s, stride)
                stride *= 2
            cnt = jnp.minimum(pos1, w).astype(F32)
            tok = e[HALO_P:]
            pooled = s[HALO_P:] * (1.0 / cnt) - tok
            pooled_ref[r0:r0 + 32, cols] = pooled.astype(BF16)
    for g in range(N_GROUPS):
        cols = slice(g * GROUP, (g + 1) * GROUP)
        y_c = _dot(pooled_ref[:, cols], pw_ref[g]) * psc_ref[:, cols]
        for r0 in range(0, T, 64):
            rows = slice(r0, r0 + 64)
            mg_ref[rows, cols] = gate(2, rows, cols) * y_c[r0:r0 + 64]

    for h in range(N_HEADS):
        cols = slice(h * HEAD, (h + 1) * HEAD)
        for r0 in range(0, T, 64):
            sc = s_ref[r0:r0 + 64, cols] * (HEAD ** -0.5)
            m = jnp.max(sc, axis=-1, keepdims=True)
            e = jnp.exp(sc - m)
            p = e * (1.0 / jnp.sum(e, axis=-1, keepdims=True))
            p_ref[r0:r0 + 64, cols] = p.astype(BF16)
    for h in range(N_HEADS):
        cols = slice(h * HEAD, (h + 1) * HEAD)
        pb_ref[:, cols] = _dot(p_ref[:, cols], vb_ref[0, 0, :, cols])

    project_gate(3)

    for r0 in range(0, T, 16):
        rows = slice(r0, r0 + 16)
        z = z_ref[rows, :]
        mu = jnp.mean(z, axis=-1, keepdims=True)
        zc = z - mu
        var = jnp.mean(zc * zc, axis=-1, keepdims=True)
        y = zc * lax.rsqrt(var + EPS) * lng_ref[...] + lnb_ref[...]
        y = y * jax.nn.sigmoid(y)
        mg_ref[rows, :] += gate(1, rows) * y

    for r0 in range(0, T, 32):
        rows = slice(r0, r0 + 32)
        ext_a[HALO_A + r0:HALO_A + r0 + 32, :] = proj_ref[rows, C_CA:C_GLU_V] * proj_ref[rows, C_HA:C_BA]
    for r0 in range(0, T, 32):
        rows = slice(r0, r0 + 32)
        for c0 in range(0, D, 256):
            cols = slice(c0, c0 + 256)
            e = ext_a[r0:r0 + 32 + HALO_A, cols]
            conv = (e[HALO_A:] * caw_ref[2:3, cols]
                    + _shift_rows(e, 1, HALO_A, 32) * caw_ref[1:2, cols]
                    + _shift_rows(e, 2, HALO_A, 32) * caw_ref[0:1, cols])
            y_a = proj_ref[rows, C_BA + c0:C_BA + c0 + 256] * conv
            mg_ref[rows, cols] += gate(0, rows, cols) * y_a

    for r0 in range(0, T, 32):
        rows = slice(r0, r0 + 32)
        mgb_ref[rows, :] = (mg_ref[rows, :] + gate(3, rows) * pb_ref[rows, 0:D]).astype(BF16)
    x1_ref[0] = x_ref[0] + _dot(mgb_ref[...], w_o_ref[...])

    ta = ext_a[T:T + HALO_A, :]
    tb = ext_b[T:T + HALO_B, :]
    tp = ext_p[T:T + HALO_P, :]
    ta_ref[0] = ta
    tb_ref[0] = tb
    tp_ref[0] = tp
    ext_a[0:HALO_A, :] = ta
    ext_b[0:HALO_B, :] = tb
    ext_p[0:HALO_P, :] = tp


def _roll_down(v, s):
    return pltpu.roll(v, s, axis=0)


def _const_spec(shape):
    nd = len(shape)
    return pl.BlockSpec(shape, lambda b, j: (0,) * nd, pipeline_mode=pl.Buffered(1))


def _mix_call(l, x, kb, vb, w_in, w_o, pw_b, nmix, caw, cbw, cbb, lng, lnb, psc, gb):
    nb, seq, _ = x.shape
    T = T_MIX
    vec = lambda a: a.reshape(1, -1)
    kv_spec = pl.BlockSpec((1, 1, N_MEM, D), lambda b, j: (l, b, 0, 0))
    hbm = pl.BlockSpec(memory_space=pl.ANY)
    return pl.pallas_call(
        functools.partial(_mix_kernel, l=l),
        grid=(nb, seq // T),
        in_specs=[pl.BlockSpec((1, T, D), lambda b, j: (b, j, 0)),
                  kv_spec, kv_spec,
                  hbm, hbm, _const_spec((N_GROUPS, GROUP, GROUP)),
                  _const_spec((1, D)), _const_spec((KA, D)), _const_spec((KB, D)),
                  _const_spec((1, D)), _const_spec((1, D)), _const_spec((1, D)), _const_spec((1, D)),
                  _const_spec((1, 4 * D))],
        out_specs=[pl.BlockSpec((1, T, D), lambda b, j: (b, j, 0)),
                   pl.BlockSpec((1, HALO_A, D), lambda b, j: (b, 0, 0)),
                   pl.BlockSpec((1, HALO_B, D), lambda b, j: (b, 0, 0)),
                   pl.BlockSpec((1, HALO_P, D), lambda b, j: (b, 0, 0))],
        out_shape=[jax.ShapeDtypeStruct((nb, seq, D), F32),
                   jax.ShapeDtypeStruct((nb, HALO_A, D), F32),
                   jax.ShapeDtypeStruct((nb, HALO_B, D), F32),
                   jax.ShapeDtypeStruct((nb, HALO_P, D), F32)],
        scratch_shapes=[pltpu.VMEM((D, D_PROJ), BF16),
                        pltpu.VMEM((D, D), BF16)]
        + _stage_scratch()
        + [pltpu.VMEM((T, D), BF16),
           pltpu.VMEM((T, 2 * D), F32),
           pltpu.VMEM((T, 8 * D), F32),
           pltpu.VMEM((HALO_A + T, D), F32),
           pltpu.VMEM((HALO_B + T, D), F32),
           pltpu.VMEM((HALO_P + T, D), F32),
           pltpu.VMEM((T, D), F32),
           pltpu.VMEM((T, D), F32),
           pltpu.VMEM((T, D), BF16),
           pltpu.VMEM((T, D), BF16),
           pltpu.VMEM((T, N_HEADS * N_MEM), BF16),
           pltpu.VMEM((T, N_HEADS * N_MEM), F32)],
        compiler_params=pltpu.CompilerParams(
            dimension_semantics=("arbitrary", "arbitrary"), vmem_limit_bytes=VMEM_LIMIT),
        name=f"mix_prompt_l{l}",
    )(x, kb, vb, w_in, w_o, pw_b, vec(nmix), caw, cbw, vec(cbb), vec(lng), vec(lnb), vec(psc),
      vec(gb))


def _ffn_kernel(*refs, l, tile, with_merge, final_norm):
    R = 32
    if with_merge:
        (x_ref, part_ref, g3_ref, ym_ref, w_o_hbm, nffn_ref, w1_hbm, w2_hbm, nfin_ref,
         o_ref, w1_ref, w2_ref, stage_ref, stage_sem, xn_ref, h_ref, w_o_ref, x1_ref) = refs
        jobs = [(w_o_hbm.at[l], w_o_ref)]
    else:
        (x1_ref, nffn_ref, w1_hbm, w2_hbm, nfin_ref,
         o_ref, w1_ref, w2_ref, stage_ref, stage_sem, xn_ref, h_ref) = refs
        jobs = []
    jobs += [(w1_hbm.at[l], w1_ref), (w2_hbm.at[l], w2_ref)]

    @pl.when(pl.program_id(0) == 0)
    def _():
        _stage_weights(jobs, stage_ref, stage_sem)

    if with_merge:
        for r0 in range(0, tile, R):
            rows = slice(r0, r0 + R)
            xn_ref[rows, :] = (part_ref[rows, :] + g3_ref[rows, :] * ym_ref[rows, :]).astype(BF16)
        x1_ref[...] = x_ref[...] + _dot(xn_ref[...], w_o_ref[...])
    g_ffn = nffn_ref[...]
    for r0 in range(0, tile, R):
        rows = slice(r0, r0 + R)
        xn_ref[rows, :] = _rms_rows(x1_ref[rows, :], g_ffn).astype(BF16)
    for f0 in range(0, D_FF, D):
        h = jnp.maximum(_dot(xn_ref[...], w1_ref[:, f0:f0 + D]), 0.0)
        h_ref[:, f0:f0 + D] = (h * h).astype(BF16)
    y = x1_ref[...] + _dot(h_ref[...], w2_ref[...])
    if final_norm:
        y = _rms_rows(y, nfin_ref[...])
    o_ref[...] = y


def _ffn_call(l, x, nffn, w1, w2, nfin, *, final_norm, merge=None, name):
    n = x.shape[0]
    tile = min(T_FFN, n)
    row = pl.BlockSpec((tile, D), lambda i: (i, 0))
    const = lambda shape: pl.BlockSpec(shape, lambda i: (0,) * len(shape), pipeline_mode=pl.Buffered(1))
    hbm = pl.BlockSpec(memory_space=pl.ANY)
    args, specs = [x], [row]
    if merge is not None:
        part, g3, ym, w_o = merge
        args += [part, g3, ym, w_o]
        specs += [row, row, row, hbm]
    args += [nffn.reshape(1, D), w1, w2, nfin.reshape(1, D)]
    specs += [const((1, D)), hbm, hbm, const((1, D))]
    return pl.pallas_call(
        functools.partial(_ffn_kernel, l=l, tile=tile, with_merge=merge is not None,
                          final_norm=final_norm),
        grid=(n // tile,),
        in_specs=specs,
        out_specs=row,
        out_shape=jax.ShapeDtypeStruct((n, D), F32),
        scratch_shapes=[pltpu.VMEM((D, D_FF), BF16), pltpu.VMEM((D_FF, D), BF16)]
        + _stage_scratch()
        + [pltpu.VMEM((tile, D), BF16), pltpu.VMEM((tile, D_FF), BF16)]
        + ([pltpu.VMEM((D, D), BF16), pltpu.VMEM((tile, D), F32)] if merge is not None else []),
        compiler_params=pltpu.CompilerParams(
            dimension_semantics=("arbitrary",), vmem_limit_bytes=VMEM_LIMIT),
        name=name,
    )(*args)


def _state_kernel(st_ref, wt_ref, st_hbm, red_ref, sh_hbm, sem, *, k_len, kb):
    l = pl.program_id(0)
    k = pl.program_id(1)
    copies = (
        pltpu.make_async_copy(st_hbm.at[l, pl.ds(1, k_len - 1)], sh_hbm.at[l, pl.ds(0, k_len - 1)],
                              sem.at[0]),
        pltpu.make_async_copy(st_hbm.at[l, pl.ds(k_len - 1, 1)], sh_hbm.at[l, pl.ds(k_len - 1, 1)],
                              sem.at[1]))

    @pl.when(k == 0)
    def _():
        for c in copies:
            c.start()

    acc = None
    for i in range(kb):
        term = st_ref[0, i] * wt_ref[0, pl.ds(k * kb + i, 1), :]
        acc = term if acc is None else acc + term

    @pl.when(k == 0)
    def _():
        red_ref[0] = acc

    @pl.when(k > 0)
    def _():
        red_ref[0] += acc

    @pl.when(k == pl.num_programs(1) - 1)
    def _():
        for c in copies:
            c.wait()


def _state_call(state_t, wt, kb, name):
    _, k_len, n, _ = state_t.shape
    assert k_len % kb == 0
    return pl.pallas_call(
        functools.partial(_state_kernel, k_len=k_len, kb=kb),
        grid=(DEPTH, k_len // kb),
        in_specs=[pl.BlockSpec((1, kb, n, D), lambda l, k: (l, k, 0, 0)),
                  pl.BlockSpec((1, k_len, D), lambda l, k: (l, 0, 0)),
                  pl.BlockSpec(memory_space=pl.ANY)],
        out_specs=[pl.BlockSpec((1, n, D), lambda l, k: (l, 0, 0)),
                   pl.BlockSpec(memory_space=pl.ANY)],
        out_shape=[jax.ShapeDtypeStruct((DEPTH, n, D), F32),
                   jax.ShapeDtypeStruct(state_t.shape, F32)],
        scratch_shapes=[pltpu.SemaphoreType.DMA((2,))],
        compiler_params=pltpu.CompilerParams(
            dimension_semantics=("arbitrary", "arbitrary"), vmem_limit_bytes=VMEM_LIMIT),
        name=name,
    )(state_t, wt, state_t)


def _mix_sample_kernel(x_ref, za_ref, zb_ref, sp_ref, w_in_hbm, pw_ref, nmix_ref, caw_ref, cbw_ref,
                       cbb_ref, lng_ref, lnb_ref, psc_ref, gb_ref, sha_in, shb_in, shp_in,
                       part_ref, q_ref, g3_ref, u_ref, glu_ref, pin_ref,
                       w_in_ref, stage_ref, stage_sem, xn_ref, proj_ref, pooled_ref, *, l, n):
    del sha_in, shb_in, shp_in
    _stage_weights([(w_in_hbm.at[l], w_in_ref)], stage_ref, stage_sem)
    R = 32
    g_mix = nmix_ref[...]
    for r0 in range(0, n, R):
        xn_ref[r0:r0 + R, :] = _rms_rows(x_ref[r0:r0 + R, :], g_mix).astype(BF16)
    for c0 in range(0, D_PROJ, D):
        proj_ref[:, c0:c0 + D] = _dot(xn_ref[...], w_in_ref[:, c0:c0 + D])

    def gate(i, rows, cols=slice(0, D)):
        c = slice(C_GATE + i * D + cols.start, C_GATE + i * D + cols.stop)
        bc = slice(i * D + cols.start, i * D + cols.stop)
        return jax.nn.sigmoid(proj_ref[rows, c] + gb_ref[:, bc])

    cnt = [float(min(PAST_LEN + 1, w)) for w in POOL_WINDOWS]
    for r0 in range(0, n, R):
        rows = slice(r0, r0 + R)
        u = proj_ref[rows, C_CA:C_GLU_V] * proj_ref[rows, C_HA:C_BA]
        u_ref[0, 0, rows, :] = u
        y_a = proj_ref[rows, C_BA:C_CA] * (za_ref[0, rows, :] + u * caw_ref[KA - 1:KA, :])
        merged = gate(0, rows) * y_a
        glu = proj_ref[rows, C_GLU_V:C_GLU_G] * jax.nn.sigmoid(proj_ref[rows, C_GLU_G:C_P])
        glu_ref[0, 0, rows, :] = glu
        z = zb_ref[0, rows, :] + glu * cbw_ref[KB - 1:KB, :] + cbb_ref[...]
        mu = jnp.mean(z, axis=-1, keepdims=True)
        zc = z - mu
        var = jnp.mean(zc * zc, axis=-1, keepdims=True)
        y = zc * lax.rsqrt(var + EPS) * lng_ref[...] + lnb_ref[...]
        y = y * jax.nn.sigmoid(y)
        merged = merged + gate(1, rows) * y
        part_ref[rows, :] = merged
        p_in = proj_ref[rows, C_P:C_Q]
        pin_ref[0, 0, rows, :] = p_in
        s = sp_ref[0, rows, :] + p_in
        for g in range(N_GROUPS):
            cols = slice(g * GROUP, (g + 1) * GROUP)
            pooled_ref[rows, cols] = (s[:, cols] * (1.0 / cnt[g]) - p_in[:, cols]).astype(BF16)
        q_ref[rows, :] = proj_ref[rows, C_Q:C_GATE]
        g3_ref[rows, :] = gate(3, rows)
    for g in range(N_GROUPS):
        cols = slice(g * GROUP, (g + 1) * GROUP)
        y_c = _dot(pooled_ref[:, cols], pw_ref[g]) * psc_ref[:, cols]
        for r0 in range(0, n, R):
            rows = slice(r0, r0 + R)
            part_ref[rows, cols] += gate(2, rows, cols) * y_c[r0:r0 + R]


def _mix_sample_call(l, x, za, zb, sp, w_in, pw_b, nmix, caw, cbw, cbb, lng, lnb, psc, gb,
                     sha, shb, shp):
    n = x.shape[0]
    vec = lambda a: a.reshape(1, -1)
    const = lambda shape: pl.BlockSpec(shape, lambda i: (0,) * len(shape), pipeline_mode=pl.Buffered(1))
    lay = pl.BlockSpec((1, n, D), lambda i: (l, 0, 0))
    row = pl.BlockSpec((n, D), lambda i: (0, 0))
    out = jax.ShapeDtypeStruct((n, D), F32)
    states = (sha, shb, shp)
    last_slot = lambda st: pl.BlockSpec((1, 1, n, D), lambda i: (l, st.shape[1] - 1, 0, 0))
    n_in = 14
    return pl.pallas_call(
        functools.partial(_mix_sample_kernel, l=l, n=n),
        grid=(1,),
        in_specs=[row, lay, lay, lay,
                  pl.BlockSpec(memory_space=pl.ANY), const((N_GROUPS, GROUP, GROUP)),
                  const((1, D)), const((KA, D)), const((KB, D)),
                  const((1, D)), const((1, D)), const((1, D)), const((1, D)), const((1, 4 * D))]
        + [pl.BlockSpec(memory_space=pl.ANY)] * 3,
        out_specs=[row] * 3 + [last_slot(st) for st in states],
        out_shape=[out] * 3 + [jax.ShapeDtypeStruct(st.shape, F32) for st in states],
        input_output_aliases={n_in: 3, n_in + 1: 4, n_in + 2: 5},
        scratch_shapes=[pltpu.VMEM((D, D_PROJ), BF16)] + _stage_scratch()
        + [pltpu.VMEM((n, D), BF16), pltpu.VMEM((n, D_PROJ), F32), pltpu.VMEM((n, D), BF16)],
        compiler_params=pltpu.CompilerParams(
            dimension_semantics=("arbitrary",), vmem_limit_bytes=VMEM_LIMIT),
        name=f"mix_sample_l{l}",
    )(x, za, zb, sp, w_in, pw_b, vec(nmix), caw, cbw, vec(cbb), vec(lng), vec(lnb), vec(psc), vec(gb),
      sha, shb, shp)


def _attn_sample_kernel(q_ref, k_ref, v_ref, o_ref, *, bt):
    half = N_MEM // 2
    pair = lambda ref, i: jnp.concatenate([ref[0, i, 0:half], ref[0, i, half:N_MEM]], axis=1)
    both = lambda a: jnp.concatenate([a, a], axis=1)
    for i in range(bt):
        q = q_ref[i]
        q8 = jnp.concatenate([q, q], axis=0)
        s = jnp.sum(pair(k_ref, i) * q8[None], axis=-1, keepdims=True) * (HEAD ** -0.5)
        m8 = jnp.max(s, axis=0, keepdims=True)
        m = both(jnp.maximum(m8[:, 0:N_HEADS], m8[:, N_HEADS:]))
        e = jnp.exp(s - m)
        d8 = jnp.sum(e, axis=0, keepdims=True)
        p = e * (1.0 / both(d8[:, 0:N_HEADS] + d8[:, N_HEADS:]))
        y8 = jnp.sum(p * pair(v_ref, i), axis=0)
        o_ref[i] = y8[0:N_HEADS] + y8[N_HEADS:]


def _attn_sample_call(l, q, cache_k, cache_v):
    n = q.shape[0]
    bt = 4
    kv = pl.BlockSpec((1, bt, N_MEM, N_HEADS, HEAD), lambda i: (l, i, 0, 0, 0))
    qo = pl.BlockSpec((bt, N_HEADS, HEAD), lambda i: (i, 0, 0))
    return pl.pallas_call(
        functools.partial(_attn_sample_kernel, bt=bt),
        grid=(n // bt,),
        in_specs=[qo, kv, kv],
        out_specs=qo,
        out_shape=jax.ShapeDtypeStruct((n, N_HEADS, HEAD), F32),
        compiler_params=pltpu.CompilerParams(
            dimension_semantics=("arbitrary",), vmem_limit_bytes=VMEM_LIMIT),
        name=f"attn_sample_l{l}",
    )(q.reshape(n, N_HEADS, HEAD), cache_k, cache_v).reshape(n, D)


def kernel(x_prompt, x_sample, mem_prompt, cache_mem_k, cache_mem_v, state_conv_a, state_conv_b,
           state_pool, norm_mix, norm_mem, w_kv, w_in, conv_a_w, conv_b_w, conv_b_bias, ln_b_gain,
           ln_b_bias, pool_w, pool_scale, gate_bias, w_o, norm_ffn, w_ff1, w_ff2, norm_final):
    nb, seq, _ = x_prompt.shape
    ns = x_sample.shape[0]

    w_kv_b = w_kv.astype(BF16)
    pw_b = pool_w.astype(BF16)

    mem_k, mem_v, kb, vb = _kv_call(mem_prompt, norm_mem, w_kv_b)
    x = x_prompt
    tails_a, tails_b, tails_p = [], [], []
    for l in range(DEPTH):
        x, ta, tb, tp = _mix_call(l, x, kb, vb, w_in, w_o, pw_b[l], norm_mix[l],
                                  conv_a_w[l], conv_b_w[l], conv_b_bias[l], ln_b_gain[l],
                                  ln_b_bias[l], pool_scale[l], gate_bias[l])
        x = _ffn_call(l, x.reshape(nb * seq, D), norm_ffn[l], w_ff1, w_ff2, norm_final,
                      final_norm=(l == DEPTH - 1), name=f"ffn_prompt_l{l}").reshape(nb, seq, D)
        tails_a.append(ta[:, HALO_A - (KA - 1):])
        tails_b.append(tb[:, HALO_B - (KB - 1):])
        tails_p.append(tp[:, HALO_P - POOL_BUF:])
    y_prompt = x
    mem_k_prompt, mem_v_prompt = mem_k, mem_v
    conv_a_prompt = jnp.stack(tails_a)
    conv_b_prompt = jnp.stack(tails_b)
    pool_prompt = jnp.stack(tails_p)

    pool_mask = jnp.concatenate(
        [jnp.broadcast_to((jnp.arange(POOL_BUF) >= POOL_BUF + 1 - w).astype(F32)[:, None], (POOL_BUF, GROUP))
         for w in POOL_WINDOWS], axis=1)
    time_major = lambda a: jnp.transpose(a, (0, 2, 1, 3))
    za, sha = _state_call(time_major(state_conv_a), conv_a_w[:, :KA - 1], 2, "state_conv_a")
    zb, shb = _state_call(time_major(state_conv_b), conv_b_w[:, :KB - 1], 6, "state_conv_b")
    sp, shp = _state_call(time_major(state_pool), jnp.broadcast_to(pool_mask, (DEPTH, POOL_BUF, D)),
                          5, "state_pool")
    xs = x_sample.reshape(ns, D)
    for l in range(DEPTH):
        part, q, g3, sha, shb, shp = _mix_sample_call(
            l, xs, za, zb, sp, w_in, pw_b[l], norm_mix[l], conv_a_w[l], conv_b_w[l],
            conv_b_bias[l], ln_b_gain[l], ln_b_bias[l], pool_scale[l], gate_bias[l], sha, shb, shp)
        ym = _attn_sample_call(l, q, cache_mem_k, cache_mem_v)
        xs = _ffn_call(l, xs, norm_ffn[l], w_ff1, w_ff2, norm_final, final_norm=(l == DEPTH - 1),
                       merge=(part, g3, ym, w_o), name=f"ffn_sample_l{l}")
    y_sample = xs.reshape(ns, 1, D)
    conv_a_sample = time_major(sha)
    conv_b_sample = time_major(shb)
    pool_sample = time_major(shp)

    return (y_prompt, y_sample, mem_k_prompt, mem_v_prompt, conv_a_prompt, conv_b_prompt,
            pool_prompt, conv_a_sample, conv_b_sample, pool_sample)
```

```python
import functools

import jax
import jax.numpy as jnp
from jax import lax
from jax.experimental import pallas as pl
from jax.experimental.pallas import tpu as pltpu

D = 1024
DEPTH = 2
N_MEM = 256
N_HEADS = 4
HEAD = D // N_HEADS
N_GROUPS = 4
GROUP = D // N_GROUPS
POOL_WINDOWS = (2, 4, 8, 16)
POOL_BUF = 15
KA = 3
KB = 31
D_FF = 4 * D
D_PROJ = 11 * D
PAST_LEN = 16384
EPS = 1e-6

C_HA, C_BA, C_CA, C_GLU_V, C_GLU_G, C_P, C_Q, C_GATE = (
    0, D, 2 * D, 3 * D, 4 * D, 5 * D, 6 * D, 7 * D)

SUBLANES = 8
LANES = 128
VMEM_LIMIT = 60 * 1024 * 1024

T_MIX = 256
T_FFN = 512
HALO_A, HALO_B, HALO_P = 8, 32, 16
STAGE_ROWS, STAGE_COLS = 256, 1024
STAGE_SLOTS = 4

BF16 = jnp.bfloat16
F32 = jnp.float32


def _dot(a, b):
    return jnp.dot(a, b, preferred_element_type=F32)


def _rms_rows(x, g):
    ms = jnp.mean(x * x, axis=-1, keepdims=True)
    return x * lax.rsqrt(ms + EPS) * g


def _stage_weights(jobs, stage_ref, sem_ref):
    chunks = [(src, dst, r0, c0) for src, dst in jobs
              for r0 in range(0, src.shape[0], STAGE_ROWS)
              for c0 in range(0, src.shape[1], STAGE_COLS)]
    ahead = STAGE_SLOTS - 1

    def copy(i):
        src, _, r0, c0 = chunks[i]
        return pltpu.make_async_copy(src.at[pl.ds(r0, STAGE_ROWS), pl.ds(c0, STAGE_COLS)],
                                     stage_ref.at[i % STAGE_SLOTS], sem_ref.at[i % STAGE_SLOTS])

    for i in range(min(ahead, len(chunks))):
        copy(i).start()
    for i, (_, dst, r0, c0) in enumerate(chunks):
        if i + ahead < len(chunks):
            copy(i + ahead).start()
        copy(i).wait()
        dst[r0:r0 + STAGE_ROWS, c0:c0 + STAGE_COLS] = stage_ref[i % STAGE_SLOTS].astype(BF16)


def _stage_scratch():
    return [pltpu.VMEM((STAGE_SLOTS, STAGE_ROWS, STAGE_COLS), F32),
            pltpu.SemaphoreType.DMA((STAGE_SLOTS,))]


def _shift_rows(v, s, halo, rows):
    return v[halo - s:halo - s + rows]


def _kv_kernel(mem_ref, g_ref, w_ref, k_ref, v_ref, kb_ref, vb_ref):
    memn = _rms_rows(mem_ref[0], g_ref[0]).astype(BF16)
    kv = _dot(memn, w_ref[0])
    k = kv[:, :D]
    v = kv[:, D:]
    k_ref[0, 0] = k.reshape(N_MEM, N_HEADS, HEAD)
    v_ref[0, 0] = v.reshape(N_MEM, N_HEADS, HEAD)
    kb_ref[0, 0] = k.astype(BF16)
    vb_ref[0, 0] = v.astype(BF16)


def _kv_call(mem, norm_mem, w_kv_b):
    nb = mem.shape[0]
    out_f = jax.ShapeDtypeStruct((DEPTH, nb, N_MEM, N_HEADS, HEAD), F32)
    out_b = jax.ShapeDtypeStruct((DEPTH, nb, N_MEM, D), BF16)
    blk = pl.BlockSpec((1, 1, N_MEM, D), lambda l, b: (l, b, 0, 0))
    blk_f = pl.BlockSpec((1, 1, N_MEM, N_HEADS, HEAD), lambda l, b: (l, b, 0, 0, 0))
    return pl.pallas_call(
        _kv_kernel,
        grid=(DEPTH, nb),
        in_specs=[pl.BlockSpec((1, N_MEM, D), lambda l, b: (b, 0, 0)),
                  pl.BlockSpec((1, 1, D), lambda l, b: (l, 0, 0)),
                  pl.BlockSpec((1, D, 2 * D), lambda l, b: (l, 0, 0))],
        out_specs=[blk_f, blk_f, blk, blk],
        out_shape=[out_f, out_f, out_b, out_b],
        compiler_params=pltpu.CompilerParams(
            dimension_semantics=("arbitrary", "arbitrary"), vmem_limit_bytes=VMEM_LIMIT),
        name="kv_proj",
    )(mem, norm_mem.reshape(DEPTH, 1, D), w_kv_b)


def _mix_kernel(x_ref, kb_ref, vb_ref, w_in_hbm, w_o_hbm, pw_ref, nmix_ref, caw_ref,
                cbw_ref, cbb_ref, lng_ref, lnb_ref, psc_ref, gb_ref,
                x1_ref, ta_ref, tb_ref, tp_ref,
                w_in_ref, w_o_ref, stage_ref, stage_sem,
                xn_ref, pb_ref, proj_ref, ext_a, ext_b, ext_p, z_ref, mg_ref, mgb_ref, pooled_ref,
                p_ref, *, l):
    T = T_MIX
    j = pl.program_id(1)

    @pl.when((pl.program_id(0) == 0) & (j == 0))
    def _():
        _stage_weights([(w_in_hbm.at[l], w_in_ref), (w_o_hbm.at[l], w_o_ref)], stage_ref, stage_sem)

    @pl.when(j == 0)
    def _():
        ext_a[0:HALO_A, :] = jnp.zeros((HALO_A, D), F32)
        ext_b[0:HALO_B, :] = jnp.zeros((HALO_B, D), F32)
        ext_p[0:HALO_P, :] = jnp.zeros((HALO_P, D), F32)

    g_mix = nmix_ref[...]
    for r0 in range(0, T, 32):
        xn_ref[r0:r0 + 32, :] = _rms_rows(x_ref[0, r0:r0 + 32, :], g_mix).astype(BF16)

    pb_ref[...] = _dot(xn_ref[...], w_in_ref[:, C_GLU_V:C_P])
    P_Q, P_GATE = 3 * D, 4 * D
    proj_ref[:, 0:P_Q] = _dot(xn_ref[...], w_in_ref[:, C_HA:C_GLU_V])
    ext_p[HALO_P:HALO_P + T, :] = _dot(xn_ref[...], w_in_ref[:, C_P:C_Q])
    proj_ref[:, P_Q:P_GATE] = _dot(xn_ref[...], w_in_ref[:, C_Q:C_GATE])
    proj_ref[:, P_GATE:P_GATE + 4 * D] = _dot(xn_ref[...], w_in_ref[:, C_GATE:D_PROJ])

    def gate(i, rows, cols=slice(0, D)):
        c = slice(P_GATE + i * D + cols.start, P_GATE + i * D + cols.stop)
        bc = slice(i * D + cols.start, i * D + cols.stop)
        return jax.nn.sigmoid(proj_ref[rows, c] + gb_ref[:, bc])

    for r0 in range(0, T, 32):
        rows = slice(r0, r0 + 32)
        glu = pb_ref[rows, 0:D] * jax.nn.sigmoid(pb_ref[rows, D:2 * D])
        ext_b[HALO_B + r0:HALO_B + r0 + 32, :] = glu

    R = 128
    for r0 in range(0, T, R):
        for c0 in range(0, D, LANES):
            cols = slice(c0, c0 + LANES)
            acc = None
            for b in range(SUBLANES):
                q = None
                for a in range(4):
                    d = SUBLANES * a + b
                    if d > KB - 1:
                        continue
                    base = HALO_B + r0 - SUBLANES - SUBLANES * a
                    term = ext_b[base:base + R + SUBLANES, cols] * cbw_ref[KB - 1 - d:KB - d, cols]
                    q = term if q is None else q + term
                sh = _shift_rows(q, b, SUBLANES, R)
                acc = sh if acc is None else acc + sh
            z_ref[r0:r0 + R, cols] = acc + cbb_ref[:, cols]
    for r0 in range(0, T, 16):
        rows = slice(r0, r0 + 16)
        z = z_ref[rows, :]
        mu = jnp.mean(z, axis=-1, keepdims=True)
        zc = z - mu
        var = jnp.mean(zc * zc, axis=-1, keepdims=True)
        y = zc * lax.rsqrt(var + EPS) * lng_ref[...] + lnb_ref[...]
        y = y * jax.nn.sigmoid(y)
        mg_ref[rows, :] = gate(1, rows) * y

    for r0 in range(0, T, 32):
        rows = slice(r0, r0 + 32)
        ext_a[HALO_A + r0:HALO_A + r0 + 32, :] = proj_ref[rows, C_CA:C_GLU_V] * proj_ref[rows, C_HA:C_BA]
    for r0 in range(0, T, 32):
        rows = slice(r0, r0 + 32)
        for c0 in range(0, D, 256):
            cols = slice(c0, c0 + 256)
            e = ext_a[r0:r0 + 32 + HALO_A, cols]
            conv = (e[HALO_A:] * caw_ref[2:3, cols]
                    + _shift_rows(e, 1, HALO_A, 32) * caw_ref[1:2, cols]
                    + _shift_rows(e, 2, HALO_A, 32) * caw_ref[0:1, cols])
            y_a = proj_ref[rows, C_BA + c0:C_BA + c0 + 256] * conv
            mg_ref[rows, cols] += gate(0, rows, cols) * y_a

    for r0 in range(0, T, 32):
        pos1 = j * T + r0 + 1 + lax.broadcasted_iota(jnp.int32, (32, 1), 0)
        for g, w in enumerate(POOL_WINDOWS):
            cols = slice(g * GROUP, (g + 1) * GROUP)
            e = ext_p[r0:r0 + 32 + HALO_P, cols]
            s = e
            stride = 1
            while stride < w:
                s = s + _roll_down(s, stride)
                stride *= 2
            cnt = jnp.minimum(pos1, w).astype(F32)
            tok = e[HALO_P:]
            pooled = s[HALO_P:] * (1.0 / cnt) - tok
            pooled_ref[r0:r0 + 32, cols] = pooled.astype(BF16)
    for g in range(N_GROUPS):
        cols = slice(g * GROUP, (g + 1) * GROUP)
        y_c = _dot(pooled_ref[:, cols], pw_ref[g]) * psc_ref[:, cols]
        for r0 in range(0, T, 64):
            rows = slice(r0, r0 + 64)
            mg_ref[rows, cols] += gate(2, rows, cols) * y_c[r0:r0 + 64]

    for h in range(N_HEADS):
        cols = slice(h * HEAD, (h + 1) * HEAD)
        qh = proj_ref[:, P_Q + h * HEAD:P_Q + (h + 1) * HEAD].astype(BF16)
        s = lax.dot_general(qh, kb_ref[0, 0, :, cols], (((1,), (1,)), ((), ())),
                            preferred_element_type=F32) * (HEAD ** -0.5)
        for r0 in range(0, T, 64):
            sc = s[r0:r0 + 64]
            m = jnp.max(sc, axis=-1, keepdims=True)
            e = jnp.exp(sc - m)
            p = e * (1.0 / jnp.sum(e, axis=-1, keepdims=True))
            p_ref[r0:r0 + 64, :] = p.astype(BF16)
        y_m = _dot(p_ref[...], vb_ref[0, 0, :, cols])
        for r0 in range(0, T, 64):
            rows = slice(r0, r0 + 64)
            mg_ref[rows, cols] += gate(3, rows, cols) * y_m[r0:r0 + 64]

    for r0 in range(0, T, 32):
        mgb_ref[r0:r0 + 32, :] = mg_ref[r0:r0 + 32, :].astype(BF16)
    x1_ref[0] = x_ref[0] + _dot(mgb_ref[...], w_o_ref[...])

    ta = ext_a[T:T + HALO_A, :]
    tb = ext_b[T:T + HALO_B, :]
    tp = ext_p[T:T + HALO_P, :]
    ta_ref[0] = ta
    tb_ref[0] = tb
    tp_ref[0] = tp
    ext_a[0:HALO_A, :] = ta
    ext_b[0:HALO_B, :] = tb
    ext_p[0:HALO_P, :] = tp


def _roll_down(v, s):
    return pltpu.roll(v, s, axis=0)


def _const_spec(shape):
    nd = len(shape)
    return pl.BlockSpec(shape, lambda b, j: (0,) * nd, pipeline_mode=pl.Buffered(1))


def _mix_call(l, x, kb, vb, w_in, w_o, pw_b, nmix, caw, cbw, cbb, lng, lnb, psc, gb):
    nb, seq, _ = x.shape
    T = T_MIX
    vec = lambda a: a.reshape(1, -1)
    kv_spec = pl.BlockSpec((1, 1, N_MEM, D), lambda b, j: (l, b, 0, 0))
    hbm = pl.BlockSpec(memory_space=pl.ANY)
    return pl.pallas_call(
        functools.partial(_mix_kernel, l=l),
        grid=(nb, seq // T),
        in_specs=[pl.BlockSpec((1, T, D), lambda b, j: (b, j, 0)),
                  kv_spec, kv_spec,
                  hbm, hbm, _const_spec((N_GROUPS, GROUP, GROUP)),
                  _const_spec((1, D)), _const_spec((KA, D)), _const_spec((KB, D)),
                  _const_spec((1, D)), _const_spec((1, D)), _const_spec((1, D)), _const_spec((1, D)),
                  _const_spec((1, 4 * D))],
        out_specs=[pl.BlockSpec((1, T, D), lambda b, j: (b, j, 0)),
                   pl.BlockSpec((1, HALO_A, D), lambda b, j: (b, 0, 0)),
                   pl.BlockSpec((1, HALO_B, D), lambda b, j: (b, 0, 0)),
                   pl.BlockSpec((1, HALO_P, D), lambda b, j: (b, 0, 0))],
        out_shape=[jax.ShapeDtypeStruct((nb, seq, D), F32),
                   jax.ShapeDtypeStruct((nb, HALO_A, D), F32),
                   jax.ShapeDtypeStruct((nb, HALO_B, D), F32),
                   jax.ShapeDtypeStruct((nb, HALO_P, D), F32)],
        scratch_shapes=[pltpu.VMEM((D, D_PROJ), BF16),
                        pltpu.VMEM((D, D), BF16)]
        + _stage_scratch()
        + [pltpu.VMEM((T, D), BF16),
           pltpu.VMEM((T, 2 * D), F32),
           pltpu.VMEM((T, 8 * D), F32),
           pltpu.VMEM((HALO_A + T, D), F32),
           pltpu.VMEM((HALO_B + T, D), F32),
           pltpu.VMEM((HALO_P + T, D), F32),
           pltpu.VMEM((T, D), F32),
           pltpu.VMEM((T, D), F32),
           pltpu.VMEM((T, D), BF16),
           pltpu.VMEM((T, D), BF16),
           pltpu.VMEM((T, N_MEM), BF16)],
        compiler_params=pltpu.CompilerParams(
            dimension_semantics=("arbitrary", "arbitrary"), vmem_limit_bytes=VMEM_LIMIT),
        name=f"mix_prompt_l{l}",
    )(x, kb, vb, w_in, w_o, pw_b, vec(nmix), caw, cbw, vec(cbb), vec(lng), vec(lnb), vec(psc),
      vec(gb))


def _ffn_kernel(*refs, l, tile, with_merge, final_norm):
    R = 32
    if with_merge:
        (x_ref, part_ref, g3_ref, ym_ref, w_o_hbm, nffn_ref, w1_hbm, w2_hbm, nfin_ref,
         o_ref, w1_ref, w2_ref, stage_ref, stage_sem, xn_ref, h_ref, w_o_ref, x1_ref) = refs
        jobs = [(w_o_hbm.at[l], w_o_ref)]
    else:
        (x1_ref, nffn_ref, w1_hbm, w2_hbm, nfin_ref,
         o_ref, w1_ref, w2_ref, stage_ref, stage_sem, xn_ref, h_ref) = refs
        jobs = []
    jobs += [(w1_hbm.at[l], w1_ref), (w2_hbm.at[l], w2_ref)]

    @pl.when(pl.program_id(0) == 0)
    def _():
        _stage_weights(jobs, stage_ref, stage_sem)

    if with_merge:
        for r0 in range(0, tile, R):
            rows = slice(r0, r0 + R)
            xn_ref[rows, :] = (part_ref[rows, :] + g3_ref[rows, :] * ym_ref[rows, :]).astype(BF16)
        x1_ref[...] = x_ref[...] + _dot(xn_ref[...], w_o_ref[...])
    g_ffn = nffn_ref[...]
    for r0 in range(0, tile, R):
        rows = slice(r0, r0 + R)
        xn_ref[rows, :] = _rms_rows(x1_ref[rows, :], g_ffn).astype(BF16)
    for f0 in range(0, D_FF, D):
        h = jnp.maximum(_dot(xn_ref[...], w1_ref[:, f0:f0 + D]), 0.0)
        h_ref[:, f0:f0 + D] = (h * h).astype(BF16)
    y = x1_ref[...] + _dot(h_ref[...], w2_ref[...])
    if final_norm:
        y = _rms_rows(y, nfin_ref[...])
    o_ref[...] = y


def _ffn_call(l, x, nffn, w1, w2, nfin, *, final_norm, merge=None, name):
    n = x.shape[0]
    tile = min(T_FFN, n)
    row = pl.BlockSpec((tile, D), lambda i: (i, 0))
    const = lambda shape: pl.BlockSpec(shape, lambda i: (0,) * len(shape), pipeline_mode=pl.Buffered(1))
    hbm = pl.BlockSpec(memory_space=pl.ANY)
    args, specs = [x], [row]
    if merge is not None:
        part, g3, ym, w_o = merge
        args += [part, g3, ym, w_o]
        specs += [row, row, row, hbm]
    args += [nffn.reshape(1, D), w1, w2, nfin.reshape(1, D)]
    specs += [const((1, D)), hbm, hbm, const((1, D))]
    return pl.pallas_call(
        functools.partial(_ffn_kernel, l=l, tile=tile, with_merge=merge is not None,
                          final_norm=final_norm),
        grid=(n // tile,),
        in_specs=specs,
        out_specs=row,
        out_shape=jax.ShapeDtypeStruct((n, D), F32),
        scratch_shapes=[pltpu.VMEM((D, D_FF), BF16), pltpu.VMEM((D_FF, D), BF16)]
        + _stage_scratch()
        + [pltpu.VMEM((tile, D), BF16), pltpu.VMEM((tile, D_FF), BF16)]
        + ([pltpu.VMEM((D, D), BF16), pltpu.VMEM((tile, D), F32)] if merge is not None else []),
        compiler_params=pltpu.CompilerParams(
            dimension_semantics=("arbitrary",), vmem_limit_bytes=VMEM_LIMIT),
        name=name,
    )(*args)


def _state_kernel(st_ref, nxt_ref, wt_ref, red_ref, sh_ref, *, kb):
    k = pl.program_id(1)
    acc = None
    for i in range(kb):
        row = st_ref[0, i]
        term = row * wt_ref[0, pl.ds(k * kb + i, 1), :]
        acc = term if acc is None else acc + term
        if i > 0:
            sh_ref[0, i - 1] = row
    sh_ref[0, kb - 1] = nxt_ref[0, 0]

    @pl.when(k == 0)
    def _():
        red_ref[0] = acc

    @pl.when(k > 0)
    def _():
        red_ref[0] += acc


def _state_call(state_t, wt, kb, name):
    _, k_len, n, _ = state_t.shape
    assert k_len % kb == 0
    blk = pl.BlockSpec((1, kb, n, D), lambda l, k: (l, k, 0, 0))
    return pl.pallas_call(
        functools.partial(_state_kernel, kb=kb),
        grid=(DEPTH, k_len // kb),
        in_specs=[blk,
                  pl.BlockSpec((1, 1, n, D), lambda l, k: (l, jnp.minimum((k + 1) * kb, k_len - 1), 0, 0)),
                  pl.BlockSpec((1, k_len, D), lambda l, k: (l, 0, 0))],
        out_specs=[pl.BlockSpec((1, n, D), lambda l, k: (l, 0, 0)), blk],
        out_shape=[jax.ShapeDtypeStruct((DEPTH, n, D), F32),
                   jax.ShapeDtypeStruct(state_t.shape, F32)],
        compiler_params=pltpu.CompilerParams(
            dimension_semantics=("arbitrary", "arbitrary"), vmem_limit_bytes=VMEM_LIMIT),
        name=name,
    )(state_t, state_t, wt)


def _mix_sample_kernel(x_ref, za_ref, zb_ref, sp_ref, w_in_hbm, pw_ref, nmix_ref, caw_ref, cbw_ref,
                       cbb_ref, lng_ref, lnb_ref, psc_ref, gb_ref, sha_in, shb_in, shp_in,
                       part_ref, q_ref, g3_ref, u_ref, glu_ref, pin_ref,
                       w_in_ref, stage_ref, stage_sem, xn_ref, proj_ref, pooled_ref, *, l, n):
    del sha_in, shb_in, shp_in
    _stage_weights([(w_in_hbm.at[l], w_in_ref)], stage_ref, stage_sem)
    R = 32
    g_mix = nmix_ref[...]
    for r0 in range(0, n, R):
        xn_ref[r0:r0 + R, :] = _rms_rows(x_ref[r0:r0 + R, :], g_mix).astype(BF16)
    for c0 in range(0, D_PROJ, D):
        proj_ref[:, c0:c0 + D] = _dot(xn_ref[...], w_in_ref[:, c0:c0 + D])

    def gate(i, rows, cols=slice(0, D)):
        c = slice(C_GATE + i * D + cols.start, C_GATE + i * D + cols.stop)
        bc = slice(i * D + cols.start, i * D + cols.stop)
        return jax.nn.sigmoid(proj_ref[rows, c] + gb_ref[:, bc])

    cnt = [float(min(PAST_LEN + 1, w)) for w in POOL_WINDOWS]
    for r0 in range(0, n, R):
        rows = slice(r0, r0 + R)
        u = proj_ref[rows, C_CA:C_GLU_V] * proj_ref[rows, C_HA:C_BA]
        u_ref[0, 0, rows, :] = u
        y_a = proj_ref[rows, C_BA:C_CA] * (za_ref[0, rows, :] + u * caw_ref[KA - 1:KA, :])
        merged = gate(0, rows) * y_a
        glu = proj_ref[rows, C_GLU_V:C_GLU_G] * jax.nn.sigmoid(proj_ref[rows, C_GLU_G:C_P])
        glu_ref[0, 0, rows, :] = glu
        z = zb_ref[0, rows, :] + glu * cbw_ref[KB - 1:KB, :] + cbb_ref[...]
        mu = jnp.mean(z, axis=-1, keepdims=True)
        zc = z - mu
        var = jnp.mean(zc * zc, axis=-1, keepdims=True)
        y = zc * lax.rsqrt(var + EPS) * lng_ref[...] + lnb_ref[...]
        y = y * jax.nn.sigmoid(y)
        merged = merged + gate(1, rows) * y
        part_ref[rows, :] = merged
        p_in = proj_ref[rows, C_P:C_Q]
        pin_ref[0, 0, rows, :] = p_in
        s = sp_ref[0, rows, :] + p_in
        for g in range(N_GROUPS):
            cols = slice(g * GROUP, (g + 1) * GROUP)
            pooled_ref[rows, cols] = (s[:, cols] * (1.0 / cnt[g]) - p_in[:, cols]).astype(BF16)
        q_ref[rows, :] = proj_ref[rows, C_Q:C_GATE]
        g3_ref[rows, :] = gate(3, rows)
    for g in range(N_GROUPS):
        cols = slice(g * GROUP, (g + 1) * GROUP)
        y_c = _dot(pooled_ref[:, cols], pw_ref[g]) * psc_ref[:, cols]
        for r0 in range(0, n, R):
            rows = slice(r0, r0 + R)
            part_ref[rows, cols] += gate(2, rows, cols) * y_c[r0:r0 + R]


def _mix_sample_call(l, x, za, zb, sp, w_in, pw_b, nmix, caw, cbw, cbb, lng, lnb, psc, gb,
                     sha, shb, shp):
    n = x.shape[0]
    vec = lambda a: a.reshape(1, -1)
    const = lambda shape: pl.BlockSpec(shape, lambda i: (0,) * len(shape), pipeline_mode=pl.Buffered(1))
    lay = pl.BlockSpec((1, n, D), lambda i: (l, 0, 0))
    row = pl.BlockSpec((n, D), lambda i: (0, 0))
    out = jax.ShapeDtypeStruct((n, D), F32)
    states = (sha, shb, shp)
    last_slot = lambda st: pl.BlockSpec((1, 1, n, D), lambda i: (l, st.shape[1] - 1, 0, 0))
    n_in = 14
    return pl.pallas_call(
        functools.partial(_mix_sample_kernel, l=l, n=n),
        grid=(1,),
        in_specs=[row, lay, lay, lay,
                  pl.BlockSpec(memory_space=pl.ANY), const((N_GROUPS, GROUP, GROUP)),
                  const((1, D)), const((KA, D)), const((KB, D)),
                  const((1, D)), const((1, D)), const((1, D)), const((1, D)), const((1, 4 * D))]
        + [pl.BlockSpec(memory_space=pl.ANY)] * 3,
        out_specs=[row] * 3 + [last_slot(st) for st in states],
        out_shape=[out] * 3 + [jax.ShapeDtypeStruct(st.shape, F32) for st in states],
        input_output_aliases={n_in: 3, n_in + 1: 4, n_in + 2: 5},
        scratch_shapes=[pltpu.VMEM((D, D_PROJ), BF16)] + _stage_scratch()
        + [pltpu.VMEM((n, D), BF16), pltpu.VMEM((n, D_PROJ), F32), pltpu.VMEM((n, D), BF16)],
        compiler_params=pltpu.CompilerParams(
            dimension_semantics=("arbitrary",), vmem_limit_bytes=VMEM_LIMIT),
        name=f"mix_sample_l{l}",
    )(x, za, zb, sp, w_in, pw_b, vec(nmix), caw, cbw, vec(cbb), vec(lng), vec(lnb), vec(psc), vec(gb),
      sha, shb, shp)


def _attn_sample_kernel(q_ref, k_ref, v_ref, o_ref, *, bt):
    half = N_MEM // 2
    pair = lambda ref, i: jnp.concatenate([ref[0, i, 0:half], ref[0, i, half:N_MEM]], axis=1)
    both = lambda a: jnp.concatenate([a, a], axis=1)
    for i in range(bt):
        q = q_ref[i]
        q8 = jnp.concatenate([q, q], axis=0)
        s = jnp.sum(pair(k_ref, i) * q8[None], axis=-1, keepdims=True) * (HEAD ** -0.5)
        m8 = jnp.max(s, axis=0, keepdims=True)
        m = both(jnp.maximum(m8[:, 0:N_HEADS], m8[:, N_HEADS:]))
        e = jnp.exp(s - m)
        d8 = jnp.sum(e, axis=0, keepdims=True)
        p = e * (1.0 / both(d8[:, 0:N_HEADS] + d8[:, N_HEADS:]))
        y8 = jnp.sum(p * pair(v_ref, i), axis=0)
        o_ref[i] = y8[0:N_HEADS] + y8[N_HEADS:]


def _attn_sample_call(l, q, cache_k, cache_v):
    n = q.shape[0]
    bt = 4
    kv = pl.BlockSpec((1, bt, N_MEM, N_HEADS, HEAD), lambda i: (l, i, 0, 0, 0))
    qo = pl.BlockSpec((bt, N_HEADS, HEAD), lambda i: (i, 0, 0))
    return pl.pallas_call(
        functools.partial(_attn_sample_kernel, bt=bt),
        grid=(n // bt,),
        in_specs=[qo, kv, kv],
        out_specs=qo,
        out_shape=jax.ShapeDtypeStruct((n, N_HEADS, HEAD), F32),
        compiler_params=pltpu.CompilerParams(
            dimension_semantics=("arbitrary",), vmem_limit_bytes=VMEM_LIMIT),
        name=f"attn_sample_l{l}",
    )(q.reshape(n, N_HEADS, HEAD), cache_k, cache_v).reshape(n, D)


def kernel(x_prompt, x_sample, mem_prompt, cache_mem_k, cache_mem_v, state_conv_a, state_conv_b,
           state_pool, norm_mix, norm_mem, w_kv, w_in, conv_a_w, conv_b_w, conv_b_bias, ln_b_gain,
           ln_b_bias, pool_w, pool_scale, gate_bias, w_o, norm_ffn, w_ff1, w_ff2, norm_final):
    nb, seq, _ = x_prompt.shape
    ns = x_sample.shape[0]

    w_kv_b = w_kv.astype(BF16)
    pw_b = pool_w.astype(BF16)

    mem_k, mem_v, kb, vb = _kv_call(mem_prompt, norm_mem, w_kv_b)
    x = x_prompt
    tails_a, tails_b, tails_p = [], [], []
    for l in range(DEPTH):
        x, ta, tb, tp = _mix_call(l, x, kb, vb, w_in, w_o, pw_b[l], norm_mix[l],
                                  conv_a_w[l], conv_b_w[l], conv_b_bias[l], ln_b_gain[l],
                                  ln_b_bias[l], pool_scale[l], gate_bias[l])
        x = _ffn_call(l, x.reshape(nb * seq, D), norm_ffn[l], w_ff1, w_ff2, norm_final,
                      final_norm=(l == DEPTH - 1), name=f"ffn_prompt_l{l}").reshape(nb, seq, D)
        tails_a.append(ta[:, HALO_A - (KA - 1):])
        tails_b.append(tb[:, HALO_B - (KB - 1):])
        tails_p.append(tp[:, HALO_P - POOL_BUF:])
    y_prompt = x
    mem_k_prompt, mem_v_prompt = mem_k, mem_v
    conv_a_prompt = jnp.stack(tails_a)
    conv_b_prompt = jnp.stack(tails_b)
    pool_prompt = jnp.stack(tails_p)

    pool_mask = jnp.concatenate(
        [jnp.broadcast_to((jnp.arange(POOL_BUF) >= POOL_BUF + 1 - w).astype(F32)[:, None], (POOL_BUF, GROUP))
         for w in POOL_WINDOWS], axis=1)
    time_major = lambda a: jnp.transpose(a, (0, 2, 1, 3))
    za, sha = _state_call(time_major(state_conv_a), conv_a_w[:, :KA - 1], 2, "state_conv_a")
    zb, shb = _state_call(time_major(state_conv_b), conv_b_w[:, :KB - 1], 6, "state_conv_b")
    sp, shp = _state_call(time_major(state_pool), jnp.broadcast_to(pool_mask, (DEPTH, POOL_BUF, D)),
                          5, "state_pool")
    xs = x_sample.reshape(ns, D)
    for l in range(DEPTH):
        part, q, g3, sha, shb, shp = _mix_sample_call(
            l, xs, za, zb, sp, w_in, pw_b[l], norm_mix[l], conv_a_w[l], conv_b_w[l],
            conv_b_bias[l], ln_b_gain[l], ln_b_bias[l], pool_scale[l], gate_bias[l], sha, shb, shp)
        ym = _attn_sample_call(l, q, cache_mem_k, cache_mem_v)
        xs = _ffn_call(l, xs, norm_ffn[l], w_ff1, w_ff2, norm_final, final_norm=(l == DEPTH - 1),
                       merge=(part, g3, ym, w_o), name=f"ffn_sample_l{l}")
    y_sample = xs.reshape(ns, 1, D)
    conv_a_sample = time_major(sha)
    conv_b_sample = time_major(shb)
    pool_sample = time_major(shp)

    return (y_prompt, y_sample, mem_k_prompt, mem_v_prompt, conv_a_prompt, conv_b_prompt,
            pool_prompt, conv_a_sample, conv_b_sample, pool_sample)
```

```python
import functools

import jax
import jax.numpy as jnp
from jax import lax
from jax.experimental import pallas as pl
from jax.experimental.pallas import tpu as pltpu

D = 1024
DEPTH = 2
N_MEM = 256
N_HEADS = 4
HEAD = D // N_HEADS
N_GROUPS = 4
GROUP = D // N_GROUPS
POOL_WINDOWS = (2, 4, 8, 16)
POOL_BUF = 15
KA = 3
KB = 31
D_FF = 4 * D
D_PROJ = 11 * D
PAST_LEN = 16384
EPS = 1e-6

C_HA, C_BA, C_CA, C_GLU_V, C_GLU_G, C_P, C_Q, C_GATE = (
    0, D, 2 * D, 3 * D, 4 * D, 5 * D, 6 * D, 7 * D)

SUBLANES = 8
LANES = 128
VMEM_LIMIT = 60 * 1024 * 1024

T_MIX = 256
T_FFN = 512
HALO_A, HALO_B, HALO_P = 8, 32, 16
STAGE_ROWS, STAGE_COLS = 256, 1024
STAGE_SLOTS = 4

BF16 = jnp.bfloat16
F32 = jnp.float32


def _dot(a, b):
    return jnp.dot(a, b, preferred_element_type=F32)


def _rms_rows(x, g):
    ms = jnp.mean(x * x, axis=-1, keepdims=True)
    return x * lax.rsqrt(ms + EPS) * g


def _stage_weights(jobs, stage_ref, sem_ref):
    chunks = [(src, dst, r0, c0) for src, dst in jobs
              for r0 in range(0, src.shape[0], STAGE_ROWS)
              for c0 in range(0, src.shape[1], STAGE_COLS)]
    ahead = STAGE_SLOTS - 1

    def copy(i):
        src, _, r0, c0 = chunks[i]
        return pltpu.make_async_copy(src.at[pl.ds(r0, STAGE_ROWS), pl.ds(c0, STAGE_COLS)],
                                     stage_ref.at[i % STAGE_SLOTS], sem_ref.at[i % STAGE_SLOTS])

    for i in range(min(ahead, len(chunks))):
        copy(i).start()
    for i, (_, dst, r0, c0) in enumerate(chunks):
        if i + ahead < len(chunks):
            copy(i + ahead).start()
        copy(i).wait()
        dst[r0:r0 + STAGE_ROWS, c0:c0 + STAGE_COLS] = stage_ref[i % STAGE_SLOTS].astype(BF16)


def _stage_scratch():
    return [pltpu.VMEM((STAGE_SLOTS, STAGE_ROWS, STAGE_COLS), F32),
            pltpu.SemaphoreType.DMA((STAGE_SLOTS,))]


def _shift_rows(v, s, halo, rows):
    return v[halo - s:halo - s + rows]


def _kv_kernel(mem_ref, g_ref, w_ref, k_ref, v_ref, kb_ref, vb_ref):
    memn = _rms_rows(mem_ref[0], g_ref[0]).astype(BF16)
    kv = _dot(memn, w_ref[0])
    k = kv[:, :D]
    v = kv[:, D:]
    k_ref[0, 0] = k.reshape(N_MEM, N_HEADS, HEAD)
    v_ref[0, 0] = v.reshape(N_MEM, N_HEADS, HEAD)
    kb_ref[0, 0] = k.astype(BF16)
    vb_ref[0, 0] = v.astype(BF16)


def _kv_call(mem, norm_mem, w_kv_b):
    nb = mem.shape[0]
    out_f = jax.ShapeDtypeStruct((DEPTH, nb, N_MEM, N_HEADS, HEAD), F32)
    out_b = jax.ShapeDtypeStruct((DEPTH, nb, N_MEM, D), BF16)
    blk = pl.BlockSpec((1, 1, N_MEM, D), lambda l, b: (l, b, 0, 0))
    blk_f = pl.BlockSpec((1, 1, N_MEM, N_HEADS, HEAD), lambda l, b: (l, b, 0, 0, 0))
    return pl.pallas_call(
        _kv_kernel,
        grid=(DEPTH, nb),
        in_specs=[pl.BlockSpec((1, N_MEM, D), lambda l, b: (b, 0, 0)),
                  pl.BlockSpec((1, 1, D), lambda l, b: (l, 0, 0)),
                  pl.BlockSpec((1, D, 2 * D), lambda l, b: (l, 0, 0))],
        out_specs=[blk_f, blk_f, blk, blk],
        out_shape=[out_f, out_f, out_b, out_b],
        compiler_params=pltpu.CompilerParams(
            dimension_semantics=("arbitrary", "arbitrary"), vmem_limit_bytes=VMEM_LIMIT),
        name="kv_proj",
    )(mem, norm_mem.reshape(DEPTH, 1, D), w_kv_b)


def _mix_kernel(x_ref, kb_ref, vb_ref, w_in_hbm, w_o_hbm, pw_ref, nmix_ref, caw_ref,
                cbw_ref, cbb_ref, lng_ref, lnb_ref, psc_ref, gb_ref,
                x1_ref, ta_ref, tb_ref, tp_ref,
                w_in_ref, w_o_ref, stage_ref, stage_sem,
                xn_ref, pb_ref, proj_ref, ext_a, ext_b, ext_p, z_ref, mg_ref, mgb_ref, pooled_ref,
                p_ref, *, l):
    T = T_MIX
    j = pl.program_id(1)

    @pl.when((pl.program_id(0) == 0) & (j == 0))
    def _():
        _stage_weights([(w_in_hbm.at[l], w_in_ref), (w_o_hbm.at[l], w_o_ref)], stage_ref, stage_sem)

    @pl.when(j == 0)
    def _():
        ext_a[0:HALO_A, :] = jnp.zeros((HALO_A, D), F32)
        ext_b[0:HALO_B, :] = jnp.zeros((HALO_B, D), F32)
        ext_p[0:HALO_P, :] = jnp.zeros((HALO_P, D), F32)

    g_mix = nmix_ref[...]
    for r0 in range(0, T, 32):
        xn_ref[r0:r0 + 32, :] = _rms_rows(x_ref[0, r0:r0 + 32, :], g_mix).astype(BF16)

    pb_ref[...] = _dot(xn_ref[...], w_in_ref[:, C_GLU_V:C_P])
    P_Q, P_GATE = 3 * D, 4 * D
    proj_ref[:, 0:P_Q] = _dot(xn_ref[...], w_in_ref[:, C_HA:C_GLU_V])
    ext_p[HALO_P:HALO_P + T, :] = _dot(xn_ref[...], w_in_ref[:, C_P:C_Q])
    proj_ref[:, P_Q:P_GATE] = _dot(xn_ref[...], w_in_ref[:, C_Q:C_GATE])
    proj_ref[:, P_GATE:P_GATE + 4 * D] = _dot(xn_ref[...], w_in_ref[:, C_GATE:D_PROJ])

    def gate(i, rows, cols=slice(0, D)):
        c = slice(P_GATE + i * D + cols.start, P_GATE + i * D + cols.stop)
        bc = slice(i * D + cols.start, i * D + cols.stop)
        return jax.nn.sigmoid(proj_ref[rows, c] + gb_ref[:, bc])

    for r0 in range(0, T, 32):
        rows = slice(r0, r0 + 32)
        glu = pb_ref[rows, 0:D] * jax.nn.sigmoid(pb_ref[rows, D:2 * D])
        ext_b[HALO_B + r0:HALO_B + r0 + 32, :] = glu

    R = 128
    for r0 in range(0, T, R):
        for c0 in range(0, D, LANES):
            cols = slice(c0, c0 + LANES)
            acc = None
            for b in range(SUBLANES):
                q = None
                for a in range(4):
                    d = SUBLANES * a + b
                    if d > KB - 1:
                        continue
                    base = HALO_B + r0 - SUBLANES - SUBLANES * a
                    term = ext_b[base:base + R + SUBLANES, cols] * cbw_ref[KB - 1 - d:KB - d, cols]
                    q = term if q is None else q + term
                sh = _shift_rows(q, b, SUBLANES, R)
                acc = sh if acc is None else acc + sh
            z_ref[r0:r0 + R, cols] = acc + cbb_ref[:, cols]
    for r0 in range(0, T, 16):
        rows = slice(r0, r0 + 16)
        z = z_ref[rows, :]
        mu = jnp.mean(z, axis=-1, keepdims=True)
        zc = z - mu
        var = jnp.mean(zc * zc, axis=-1, keepdims=True)
        y = zc * lax.rsqrt(var + EPS) * lng_ref[...] + lnb_ref[...]
        y = y * jax.nn.sigmoid(y)
        mg_ref[rows, :] = gate(1, rows) * y

    for r0 in range(0, T, 32):
        rows = slice(r0, r0 + 32)
        ext_a[HALO_A + r0:HALO_A + r0 + 32, :] = proj_ref[rows, C_CA:C_GLU_V] * proj_ref[rows, C_HA:C_BA]
    for r0 in range(0, T, 32):
        rows = slice(r0, r0 + 32)
        for c0 in range(0, D, 256):
            cols = slice(c0, c0 + 256)
            e = ext_a[r0:r0 + 32 + HALO_A, cols]
            conv = (e[HALO_A:] * caw_ref[2:3, cols]
                    + _shift_rows(e, 1, HALO_A, 32) * caw_ref[1:2, cols]
                    + _shift_rows(e, 2, HALO_A, 32) * caw_ref[0:1, cols])
            y_a = proj_ref[rows, C_BA + c0:C_BA + c0 + 256] * conv
            mg_ref[rows, cols] += gate(0, rows, cols) * y_a

    for r0 in range(0, T, 32):
        pos1 = j * T + r0 + 1 + lax.broadcasted_iota(jnp.int32, (32, 1), 0)
        for g, w in enumerate(POOL_WINDOWS):
            cols = slice(g * GROUP, (g + 1) * GROUP)
            e = ext_p[r0:r0 + 32 + HALO_P, cols]
            s = e
            stride = 1
            while stride < w:
                s = s + _roll_down(s, stride)
                stride *= 2
            cnt = jnp.minimum(pos1, w).astype(F32)
            tok = e[HALO_P:]
            pooled = s[HALO_P:] * (1.0 / cnt) - tok
            pooled_ref[r0:r0 + 32, cols] = pooled.astype(BF16)
    for g in range(N_GROUPS):
        cols = slice(g * GROUP, (g + 1) * GROUP)
        y_c = _dot(pooled_ref[:, cols], pw_ref[g]) * psc_ref[:, cols]
        for r0 in range(0, T, 64):
            rows = slice(r0, r0 + 64)
            mg_ref[rows, cols] += gate(2, rows, cols) * y_c[r0:r0 + 64]

    for h in range(N_HEADS):
        cols = slice(h * HEAD, (h + 1) * HEAD)
        qh = proj_ref[:, P_Q + h * HEAD:P_Q + (h + 1) * HEAD].astype(BF16)
        s = lax.dot_general(qh, kb_ref[0, 0, :, cols], (((1,), (1,)), ((), ())),
                            preferred_element_type=F32) * (HEAD ** -0.5)
        for r0 in range(0, T, 64):
            sc = s[r0:r0 + 64]
            m = jnp.max(sc, axis=-1, keepdims=True)
            e = jnp.exp(sc - m)
            p = e * (1.0 / jnp.sum(e, axis=-1, keepdims=True))
            p_ref[r0:r0 + 64, :] = p.astype(BF16)
        y_m = _dot(p_ref[...], vb_ref[0, 0, :, cols])
        for r0 in range(0, T, 64):
            rows = slice(r0, r0 + 64)
            mg_ref[rows, cols] += gate(3, rows, cols) * y_m[r0:r0 + 64]

    for r0 in range(0, T, 32):
        mgb_ref[r0:r0 + 32, :] = mg_ref[r0:r0 + 32, :].astype(BF16)
    x1_ref[0] = x_ref[0] + _dot(mgb_ref[...], w_o_ref[...])

    ta = ext_a[T:T + HALO_A, :]
    tb = ext_b[T:T + HALO_B, :]
    tp = ext_p[T:T + HALO_P, :]
    ta_ref[0] = ta
    tb_ref[0] = tb
    tp_ref[0] = tp
    ext_a[0:HALO_A, :] = ta
    ext_b[0:HALO_B, :] = tb
    ext_p[0:HALO_P, :] = tp


def _roll_down(v, s):
    return pltpu.roll(v, s, axis=0)


def _const_spec(shape):
    nd = len(shape)
    return pl.BlockSpec(shape, lambda b, j: (0,) * nd, pipeline_mode=pl.Buffered(1))


def _mix_call(l, x, kb, vb, w_in, w_o, pw_b, nmix, caw, cbw, cbb, lng, lnb, psc, gb):
    nb, seq, _ = x.shape
    T = T_MIX
    vec = lambda a: a.reshape(1, -1)
    kv_spec = pl.BlockSpec((1, 1, N_MEM, D), lambda b, j: (l, b, 0, 0))
    hbm = pl.BlockSpec(memory_space=pl.ANY)
    return pl.pallas_call(
        functools.partial(_mix_kernel, l=l),
        grid=(nb, seq // T),
        in_specs=[pl.BlockSpec((1, T, D), lambda b, j: (b, j, 0)),
                  kv_spec, kv_spec,
                  hbm, hbm, _const_spec((N_GROUPS, GROUP, GROUP)),
                  _const_spec((1, D)), _const_spec((KA, D)), _const_spec((KB, D)),
                  _const_spec((1, D)), _const_spec((1, D)), _const_spec((1, D)), _const_spec((1, D)),
                  _const_spec((1, 4 * D))],
        out_specs=[pl.BlockSpec((1, T, D), lambda b, j: (b, j, 0)),
                   pl.BlockSpec((1, HALO_A, D), lambda b, j: (b, 0, 0)),
                   pl.BlockSpec((1, HALO_B, D), lambda b, j: (b, 0, 0)),
                   pl.BlockSpec((1, HALO_P, D), lambda b, j: (b, 0, 0))],
        out_shape=[jax.ShapeDtypeStruct((nb, seq, D), F32),
                   jax.ShapeDtypeStruct((nb, HALO_A, D), F32),
                   jax.ShapeDtypeStruct((nb, HALO_B, D), F32),
                   jax.ShapeDtypeStruct((nb, HALO_P, D), F32)],
        scratch_shapes=[pltpu.VMEM((D, D_PROJ), BF16),
                        pltpu.VMEM((D, D), BF16)]
        + _stage_scratch()
        + [pltpu.VMEM((T, D), BF16),
           pltpu.VMEM((T, 2 * D), F32),
           pltpu.VMEM((T, 8 * D), F32),
           pltpu.VMEM((HALO_A + T, D), F32),
           pltpu.VMEM((HALO_B + T, D), F32),
           pltpu.VMEM((HALO_P + T, D), F32),
           pltpu.VMEM((T, D), F32),
           pltpu.VMEM((T, D), F32),
           pltpu.VMEM((T, D), BF16),
           pltpu.VMEM((T, D), BF16),
           pltpu.VMEM((T, N_MEM), BF16)],
        compiler_params=pltpu.CompilerParams(
            dimension_semantics=("arbitrary", "arbitrary"), vmem_limit_bytes=VMEM_LIMIT),
        name=f"mix_prompt_l{l}",
    )(x, kb, vb, w_in, w_o, pw_b, vec(nmix), caw, cbw, vec(cbb), vec(lng), vec(lnb), vec(psc),
      vec(gb))


def _ffn_kernel(*refs, l, tile, with_merge, final_norm, attn_bt):
    R = 32
    if with_merge:
        (x_ref, part_ref, g3_ref, ym_ref, w_o_hbm, nffn_ref, w1_hbm, w2_hbm, nfin_ref,
         o_ref, w1_ref, w2_ref, stage_ref, stage_sem, xn_ref, h_ref, w_o_ref, x1_ref) = refs
        jobs = [(w_o_hbm.at[l], w_o_ref)]
    elif attn_bt:
        (x1_ref, nffn_ref, w1_hbm, w2_hbm, nfin_ref, q_ref, k_ref, v_ref,
         o_ref, att_ref, w1_ref, w2_ref, stage_ref, stage_sem, xn_ref, h_ref) = refs
        jobs = []
    else:
        (x1_ref, nffn_ref, w1_hbm, w2_hbm, nfin_ref,
         o_ref, w1_ref, w2_ref, stage_ref, stage_sem, xn_ref, h_ref) = refs
        jobs = []
    jobs += [(w1_hbm.at[l], w1_ref), (w2_hbm.at[l], w2_ref)]

    @pl.when(pl.program_id(0) == 0)
    def _():
        _stage_weights(jobs, stage_ref, stage_sem)

    if attn_bt:
        _decode_attention(q_ref, k_ref, v_ref, att_ref, attn_bt)

    if with_merge:
        for r0 in range(0, tile, R):
            rows = slice(r0, r0 + R)
            xn_ref[rows, :] = (part_ref[rows, :] + g3_ref[rows, :] * ym_ref[rows, :]).astype(BF16)
        x1_ref[...] = x_ref[...] + _dot(xn_ref[...], w_o_ref[...])
    g_ffn = nffn_ref[...]
    for r0 in range(0, tile, R):
        rows = slice(r0, r0 + R)
        xn_ref[rows, :] = _rms_rows(x1_ref[rows, :], g_ffn).astype(BF16)
    for f0 in range(0, D_FF, D):
        h = jnp.maximum(_dot(xn_ref[...], w1_ref[:, f0:f0 + D]), 0.0)
        h_ref[:, f0:f0 + D] = (h * h).astype(BF16)
    y = x1_ref[...] + _dot(h_ref[...], w2_ref[...])
    if final_norm:
        y = _rms_rows(y, nfin_ref[...])
    o_ref[...] = y


def _ffn_call(l, x, nffn, w1, w2, nfin, *, final_norm, merge=None, attn=None, name):
    n = x.shape[0]
    tile = min(T_FFN, n)
    steps = n // tile
    row = pl.BlockSpec((tile, D), lambda i: (i, 0))
    const = lambda shape: pl.BlockSpec(shape, lambda i: (0,) * len(shape), pipeline_mode=pl.Buffered(1))
    hbm = pl.BlockSpec(memory_space=pl.ANY)
    args, specs = [x], [row]
    out_specs, out_shape, attn_bt = row, jax.ShapeDtypeStruct((n, D), F32), 0
    if merge is not None:
        part, g3, ym, w_o = merge
        args += [part, g3, ym, w_o]
        specs += [row, row, row, hbm]
    args += [nffn.reshape(1, D), w1, w2, nfin.reshape(1, D)]
    specs += [const((1, D)), hbm, hbm, const((1, D))]
    if attn is not None:
        q, cache_k, cache_v = attn
        ns = q.shape[0]
        attn_bt = ns // steps
        assert attn_bt * steps == ns
        kv = pl.BlockSpec((1, attn_bt, N_MEM, N_HEADS, HEAD), lambda i: (l, i, 0, 0, 0))
        qo = pl.BlockSpec((attn_bt, N_HEADS, HEAD), lambda i: (i, 0, 0))
        args += [q.reshape(ns, N_HEADS, HEAD), cache_k, cache_v]
        specs += [qo, kv, kv]
        out_specs = [row, qo]
        out_shape = [out_shape, jax.ShapeDtypeStruct((ns, N_HEADS, HEAD), F32)]
    return pl.pallas_call(
        functools.partial(_ffn_kernel, l=l, tile=tile, with_merge=merge is not None,
                          final_norm=final_norm, attn_bt=attn_bt),
        grid=(steps,),
        in_specs=specs,
        out_specs=out_specs,
        out_shape=out_shape,
        scratch_shapes=[pltpu.VMEM((D, D_FF), BF16), pltpu.VMEM((D_FF, D), BF16)]
        + _stage_scratch()
        + [pltpu.VMEM((tile, D), BF16), pltpu.VMEM((tile, D_FF), BF16)]
        + ([pltpu.VMEM((D, D), BF16), pltpu.VMEM((tile, D), F32)] if merge is not None else []),
        compiler_params=pltpu.CompilerParams(
            dimension_semantics=("arbitrary",), vmem_limit_bytes=VMEM_LIMIT),
        name=name,
    )(*args)


def _state_kernel(st_ref, nxt_ref, wt_ref, red_ref, sh_ref, *, kb):
    k = pl.program_id(1)
    acc = None
    for i in range(kb):
        row = st_ref[0, i]
        term = row * wt_ref[0, pl.ds(k * kb + i, 1), :]
        acc = term if acc is None else acc + term
        if i > 0:
            sh_ref[0, i - 1] = row
    sh_ref[0, kb - 1] = nxt_ref[0, 0]

    @pl.when(k == 0)
    def _():
        red_ref[0] = acc

    @pl.when(k > 0)
    def _():
        red_ref[0] += acc


def _state_call(state_t, wt, kb, name):
    _, k_len, n, _ = state_t.shape
    assert k_len % kb == 0
    blk = pl.BlockSpec((1, kb, n, D), lambda l, k: (l, k, 0, 0))
    return pl.pallas_call(
        functools.partial(_state_kernel, kb=kb),
        grid=(DEPTH, k_len // kb),
        in_specs=[blk,
                  pl.BlockSpec((1, 1, n, D), lambda l, k: (l, jnp.minimum((k + 1) * kb, k_len - 1), 0, 0)),
                  pl.BlockSpec((1, k_len, D), lambda l, k: (l, 0, 0))],
        out_specs=[pl.BlockSpec((1, n, D), lambda l, k: (l, 0, 0)), blk],
        out_shape=[jax.ShapeDtypeStruct((DEPTH, n, D), F32),
                   jax.ShapeDtypeStruct(state_t.shape, F32)],
        compiler_params=pltpu.CompilerParams(
            dimension_semantics=("arbitrary", "arbitrary"), vmem_limit_bytes=VMEM_LIMIT),
        name=name,
    )(state_t, state_t, wt)


def _mix_sample_kernel(x_ref, za_ref, zb_ref, sp_ref, w_in_hbm, pw_ref, nmix_ref, caw_ref, cbw_ref,
                       cbb_ref, lng_ref, lnb_ref, psc_ref, gb_ref, sha_in, shb_in, shp_in,
                       part_ref, q_ref, g3_ref, u_ref, glu_ref, pin_ref,
                       w_in_ref, stage_ref, stage_sem, xn_ref, proj_ref, pooled_ref, *, l, n):
    del sha_in, shb_in, shp_in
    _stage_weights([(w_in_hbm.at[l], w_in_ref)], stage_ref, stage_sem)
    R = 32
    g_mix = nmix_ref[...]
    for r0 in range(0, n, R):
        xn_ref[r0:r0 + R, :] = _rms_rows(x_ref[r0:r0 + R, :], g_mix).astype(BF16)
    for c0 in range(0, D_PROJ, D):
        proj_ref[:, c0:c0 + D] = _dot(xn_ref[...], w_in_ref[:, c0:c0 + D])

    def gate(i, rows, cols=slice(0, D)):
        c = slice(C_GATE + i * D + cols.start, C_GATE + i * D + cols.stop)
        bc = slice(i * D + cols.start, i * D + cols.stop)
        return jax.nn.sigmoid(proj_ref[rows, c] + gb_ref[:, bc])

    cnt = [float(min(PAST_LEN + 1, w)) for w in POOL_WINDOWS]
    for r0 in range(0, n, R):
        rows = slice(r0, r0 + R)
        u = proj_ref[rows, C_CA:C_GLU_V] * proj_ref[rows, C_HA:C_BA]
        u_ref[0, 0, rows, :] = u
        y_a = proj_ref[rows, C_BA:C_CA] * (za_ref[0, rows, :] + u * caw_ref[KA - 1:KA, :])
        merged = gate(0, rows) * y_a
        glu = proj_ref[rows, C_GLU_V:C_GLU_G] * jax.nn.sigmoid(proj_ref[rows, C_GLU_G:C_P])
        glu_ref[0, 0, rows, :] = glu
        z = zb_ref[0, rows, :] + glu * cbw_ref[KB - 1:KB, :] + cbb_ref[...]
        mu = jnp.mean(z, axis=-1, keepdims=True)
        zc = z - mu
        var = jnp.mean(zc * zc, axis=-1, keepdims=True)
        y = zc * lax.rsqrt(var + EPS) * lng_ref[...] + lnb_ref[...]
        y = y * jax.nn.sigmoid(y)
        merged = merged + gate(1, rows) * y
        part_ref[rows, :] = merged
        p_in = proj_ref[rows, C_P:C_Q]
        pin_ref[0, 0, rows, :] = p_in
        s = sp_ref[0, rows, :] + p_in
        for g in range(N_GROUPS):
            cols = slice(g * GROUP, (g + 1) * GROUP)
            pooled_ref[rows, cols] = (s[:, cols] * (1.0 / cnt[g]) - p_in[:, cols]).astype(BF16)
        q_ref[rows, :] = proj_ref[rows, C_Q:C_GATE]
        g3_ref[rows, :] = gate(3, rows)
    for g in range(N_GROUPS):
        cols = slice(g * GROUP, (g + 1) * GROUP)
        y_c = _dot(pooled_ref[:, cols], pw_ref[g]) * psc_ref[:, cols]
        for r0 in range(0, n, R):
            rows = slice(r0, r0 + R)
            part_ref[rows, cols] += gate(2, rows, cols) * y_c[r0:r0 + R]


def _mix_sample_call(l, x, za, zb, sp, w_in, pw_b, nmix, caw, cbw, cbb, lng, lnb, psc, gb,
                     sha, shb, shp):
    n = x.shape[0]
    vec = lambda a: a.reshape(1, -1)
    const = lambda shape: pl.BlockSpec(shape, lambda i: (0,) * len(shape), pipeline_mode=pl.Buffered(1))
    lay = pl.BlockSpec((1, n, D), lambda i: (l, 0, 0))
    row = pl.BlockSpec((n, D), lambda i: (0, 0))
    out = jax.ShapeDtypeStruct((n, D), F32)
    states = (sha, shb, shp)
    last_slot = lambda st: pl.BlockSpec((1, 1, n, D), lambda i: (l, st.shape[1] - 1, 0, 0))
    n_in = 14
    return pl.pallas_call(
        functools.partial(_mix_sample_kernel, l=l, n=n),
        grid=(1,),
        in_specs=[row, lay, lay, lay,
                  pl.BlockSpec(memory_space=pl.ANY), const((N_GROUPS, GROUP, GROUP)),
                  const((1, D)), const((KA, D)), const((KB, D)),
                  const((1, D)), const((1, D)), const((1, D)), const((1, D)), const((1, 4 * D))]
        + [pl.BlockSpec(memory_space=pl.ANY)] * 3,
        out_specs=[row] * 3 + [last_slot(st) for st in states],
        out_shape=[out] * 3 + [jax.ShapeDtypeStruct(st.shape, F32) for st in states],
        input_output_aliases={n_in: 3, n_in + 1: 4, n_in + 2: 5},
        scratch_shapes=[pltpu.VMEM((D, D_PROJ), BF16)] + _stage_scratch()
        + [pltpu.VMEM((n, D), BF16), pltpu.VMEM((n, D_PROJ), F32), pltpu.VMEM((n, D), BF16)],
        compiler_params=pltpu.CompilerParams(
            dimension_semantics=("arbitrary",), vmem_limit_bytes=VMEM_LIMIT),
        name=f"mix_sample_l{l}",
    )(x, za, zb, sp, w_in, pw_b, vec(nmix), caw, cbw, vec(cbb), vec(lng), vec(lnb), vec(psc), vec(gb),
      sha, shb, shp)


def _decode_attention(q_ref, k_ref, v_ref, o_ref, bt):
    half = N_MEM // 2
    pair = lambda ref, i: jnp.concatenate([ref[0, i, 0:half], ref[0, i, half:N_MEM]], axis=1)
    both = lambda a: jnp.concatenate([a, a], axis=1)
    for i in range(bt):
        q = q_ref[i]
        q8 = jnp.concatenate([q, q], axis=0)
        s = jnp.sum(pair(k_ref, i) * q8[None], axis=-1, keepdims=True) * (HEAD ** -0.5)
        m8 = jnp.max(s, axis=0, keepdims=True)
        m = both(jnp.maximum(m8[:, 0:N_HEADS], m8[:, N_HEADS:]))
        e = jnp.exp(s - m)
        d8 = jnp.sum(e, axis=0, keepdims=True)
        p = e * (1.0 / both(d8[:, 0:N_HEADS] + d8[:, N_HEADS:]))
        y8 = jnp.sum(p * pair(v_ref, i), axis=0)
        o_ref[i] = y8[0:N_HEADS] + y8[N_HEADS:]


def kernel(x_prompt, x_sample, mem_prompt, cache_mem_k, cache_mem_v, state_conv_a, state_conv_b,
           state_pool, norm_mix, norm_mem, w_kv, w_in, conv_a_w, conv_b_w, conv_b_bias, ln_b_gain,
           ln_b_bias, pool_w, pool_scale, gate_bias, w_o, norm_ffn, w_ff1, w_ff2, norm_final):
    nb, seq, _ = x_prompt.shape
    ns = x_sample.shape[0]

    w_kv_b = w_kv.astype(BF16)
    pw_b = pool_w.astype(BF16)

    mem_k_prompt, mem_v_prompt, kb, vb = _kv_call(mem_prompt, norm_mem, w_kv_b)

    pool_mask = jnp.concatenate(
        [jnp.broadcast_to((jnp.arange(POOL_BUF) >= POOL_BUF + 1 - w).astype(F32)[:, None], (POOL_BUF, GROUP))
         for w in POOL_WINDOWS], axis=1)
    time_major = lambda a: jnp.transpose(a, (0, 2, 1, 3))
    za, sha = _state_call(time_major(state_conv_a), conv_a_w[:, :KA - 1], 2, "state_conv_a")
    zb, shb = _state_call(time_major(state_conv_b), conv_b_w[:, :KB - 1], 6, "state_conv_b")
    sp, shp = _state_call(time_major(state_pool), jnp.broadcast_to(pool_mask, (DEPTH, POOL_BUF, D)),
                          5, "state_pool")
    x = x_prompt
    xs = x_sample.reshape(ns, D)
    tails_a, tails_b, tails_p = [], [], []
    for l in range(DEPTH):
        layer = (norm_mix[l], conv_a_w[l], conv_b_w[l], conv_b_bias[l], ln_b_gain[l], ln_b_bias[l],
                 pool_scale[l], gate_bias[l])
        x, ta, tb, tp = _mix_call(l, x, kb, vb, w_in, w_o, pw_b[l], *layer)
        part, q, g3, sha, shb, shp = _mix_sample_call(l, xs, za, zb, sp, w_in, pw_b[l], *layer,
                                                      sha, shb, shp)
        x, ym = _ffn_call(l, x.reshape(nb * seq, D), norm_ffn[l], w_ff1, w_ff2, norm_final,
                          final_norm=(l == DEPTH - 1), attn=(q, cache_mem_k, cache_mem_v),
                          name=f"ffn_prompt_l{l}")
        x = x.reshape(nb, seq, D)
        xs = _ffn_call(l, xs, norm_ffn[l], w_ff1, w_ff2, norm_final, final_norm=(l == DEPTH - 1),
                       merge=(part, g3, ym.reshape(ns, D), w_o), name=f"ffn_sample_l{l}")
        tails_a.append(ta[:, HALO_A - (KA - 1):])
        tails_b.append(tb[:, HALO_B - (KB - 1):])
        tails_p.append(tp[:, HALO_P - POOL_BUF:])
    y_prompt = x
    conv_a_prompt = jnp.stack(tails_a)
    conv_b_prompt = jnp.stack(tails_b)
    pool_prompt = jnp.stack(tails_p)
    y_sample = xs.reshape(ns, 1, D)
    conv_a_sample = time_major(sha)
    conv_b_sample = time_major(shb)
    pool_sample = time_major(shp)

    return (y_prompt, y_sample, mem_k_prompt, mem_v_prompt, conv_a_prompt, conv_b_prompt,
            pool_prompt, conv_a_sample, conv_b_sample, pool_sample)
```

```python
import functools

import jax
import jax.numpy as jnp
from jax import lax
from jax.experimental import pallas as pl
from jax.experimental.pallas import tpu as pltpu

D = 1024
DEPTH = 2
N_MEM = 256
N_HEADS = 4
HEAD = D // N_HEADS
N_GROUPS = 4
GROUP = D // N_GROUPS
POOL_WINDOWS = (2, 4, 8, 16)
POOL_BUF = 15
KA = 3
KB = 31
D_FF = 4 * D
D_PROJ = 11 * D
PAST_LEN = 16384
EPS = 1e-6

C_HA, C_BA, C_CA, C_GLU_V, C_GLU_G, C_P, C_Q, C_GATE = (
    0, D, 2 * D, 3 * D, 4 * D, 5 * D, 6 * D, 7 * D)

SUBLANES = 8
LANES = 128
VMEM_LIMIT = 60 * 1024 * 1024

T_MIX = 256
T_FFN = 512
HALO_A, HALO_B, HALO_P = 8, 32, 16
STAGE_ROWS, STAGE_COLS = 256, 1024
STAGE_SLOTS = 4

BF16 = jnp.bfloat16
F32 = jnp.float32


def _dot(a, b):
    return jnp.dot(a, b, preferred_element_type=F32)


def _rms_rows(x, g):
    ms = jnp.mean(x * x, axis=-1, keepdims=True)
    return x * lax.rsqrt(ms + EPS) * g


def _stage_weights(jobs, stage_ref, sem_ref):
    chunks = [(src, dst, r0, c0) for src, dst in jobs
              for r0 in range(0, src.shape[0], STAGE_ROWS)
              for c0 in range(0, src.shape[1], STAGE_COLS)]
    ahead = STAGE_SLOTS - 1

    def copy(i):
        src, _, r0, c0 = chunks[i]
        return pltpu.make_async_copy(src.at[pl.ds(r0, STAGE_ROWS), pl.ds(c0, STAGE_COLS)],
                                     stage_ref.at[i % STAGE_SLOTS], sem_ref.at[i % STAGE_SLOTS])

    for i in range(min(ahead, len(chunks))):
        copy(i).start()
    for i, (_, dst, r0, c0) in enumerate(chunks):
        if i + ahead < len(chunks):
            copy(i + ahead).start()
        copy(i).wait()
        dst[r0:r0 + STAGE_ROWS, c0:c0 + STAGE_COLS] = stage_ref[i % STAGE_SLOTS].astype(BF16)


def _stage_scratch():
    return [pltpu.VMEM((STAGE_SLOTS, STAGE_ROWS, STAGE_COLS), F32),
            pltpu.SemaphoreType.DMA((STAGE_SLOTS,))]


def _shift_rows(v, s, halo, rows):
    return v[halo - s:halo - s + rows]


def _kv_kernel(mem_ref, g_ref, w_ref, k_ref, v_ref, kb_ref, vb_ref, *, bb):
    mem = mem_ref[...].reshape(bb * N_MEM, D)
    memn = _rms_rows(mem, g_ref[0]).astype(BF16)
    kv = _dot(memn, w_ref[0])
    for r in range(bb):
        k = kv[r * N_MEM:(r + 1) * N_MEM, :D]
        v = kv[r * N_MEM:(r + 1) * N_MEM, D:]
        k_ref[0, r] = k.reshape(N_MEM, N_HEADS, HEAD)
        v_ref[0, r] = v.reshape(N_MEM, N_HEADS, HEAD)
        kb_ref[0, r] = k.astype(BF16)
        vb_ref[0, r] = v.astype(BF16)


def _kv_call(mem, norm_mem, w_kv_b):
    nb = mem.shape[0]
    bb = 2
    out_f = jax.ShapeDtypeStruct((DEPTH, nb, N_MEM, N_HEADS, HEAD), F32)
    out_b = jax.ShapeDtypeStruct((DEPTH, nb, N_MEM, D), BF16)
    blk = pl.BlockSpec((1, bb, N_MEM, D), lambda l, b: (l, b, 0, 0))
    blk_f = pl.BlockSpec((1, bb, N_MEM, N_HEADS, HEAD), lambda l, b: (l, b, 0, 0, 0))
    return pl.pallas_call(
        functools.partial(_kv_kernel, bb=bb),
        grid=(DEPTH, nb // bb),
        in_specs=[pl.BlockSpec((bb, N_MEM, D), lambda l, b: (b, 0, 0)),
                  pl.BlockSpec((1, 1, D), lambda l, b: (l, 0, 0)),
                  pl.BlockSpec((1, D, 2 * D), lambda l, b: (l, 0, 0))],
        out_specs=[blk_f, blk_f, blk, blk],
        out_shape=[out_f, out_f, out_b, out_b],
        compiler_params=pltpu.CompilerParams(
            dimension_semantics=("arbitrary", "arbitrary"), vmem_limit_bytes=VMEM_LIMIT),
        name="kv_proj",
    )(mem, norm_mem.reshape(DEPTH, 1, D), w_kv_b)


def _mix_kernel(x_ref, kb_ref, vb_ref, w_in_hbm, w_o_hbm, pw_ref, nmix_ref, caw_ref,
                cbw_ref, cbb_ref, lng_ref, lnb_ref, psc_ref, gb_ref,
                x1_ref, ta_ref, tb_ref, tp_ref,
                w_in_ref, w_o_ref, stage_ref, stage_sem,
                xn_ref, pb_ref, proj_ref, ext_a, ext_b, ext_p, z_ref, mg_ref, mgb_ref, pooled_ref,
                p_ref, *, l):
    T = T_MIX
    j = pl.program_id(1)

    @pl.when((pl.program_id(0) == 0) & (j == 0))
    def _():
        _stage_weights([(w_in_hbm.at[l], w_in_ref), (w_o_hbm.at[l], w_o_ref)], stage_ref, stage_sem)

    @pl.when(j == 0)
    def _():
        ext_a[0:HALO_A, :] = jnp.zeros((HALO_A, D), F32)
        ext_b[0:HALO_B, :] = jnp.zeros((HALO_B, D), F32)
        ext_p[0:HALO_P, :] = jnp.zeros((HALO_P, D), F32)

    g_mix = nmix_ref[...]
    for r0 in range(0, T, 32):
        xn_ref[r0:r0 + 32, :] = _rms_rows(x_ref[0, r0:r0 + 32, :], g_mix).astype(BF16)

    pb_ref[...] = _dot(xn_ref[...], w_in_ref[:, C_GLU_V:C_P])
    P_Q, P_GATE = 3 * D, 4 * D
    proj_ref[:, 0:P_Q] = _dot(xn_ref[...], w_in_ref[:, C_HA:C_GLU_V])
    ext_p[HALO_P:HALO_P + T, :] = _dot(xn_ref[...], w_in_ref[:, C_P:C_Q])
    proj_ref[:, P_Q:P_GATE] = _dot(xn_ref[...], w_in_ref[:, C_Q:C_GATE])
    proj_ref[:, P_GATE:P_GATE + 4 * D] = _dot(xn_ref[...], w_in_ref[:, C_GATE:D_PROJ])

    def gate(i, rows, cols=slice(0, D)):
        c = slice(P_GATE + i * D + cols.start, P_GATE + i * D + cols.stop)
        bc = slice(i * D + cols.start, i * D + cols.stop)
        return jax.nn.sigmoid(proj_ref[rows, c] + gb_ref[:, bc])

    for r0 in range(0, T, 32):
        rows = slice(r0, r0 + 32)
        glu = pb_ref[rows, 0:D] * jax.nn.sigmoid(pb_ref[rows, D:2 * D])
        ext_b[HALO_B + r0:HALO_B + r0 + 32, :] = glu

    R = 128
    for r0 in range(0, T, R):
        for c0 in range(0, D, LANES):
            cols = slice(c0, c0 + LANES)
            acc = None
            for b in range(SUBLANES):
                q = None
                for a in range(4):
                    d = SUBLANES * a + b
                    if d > KB - 1:
                        continue
                    base = HALO_B + r0 - SUBLANES - SUBLANES * a
                    term = ext_b[base:base + R + SUBLANES, cols] * cbw_ref[KB - 1 - d:KB - d, cols]
                    q = term if q is None else q + term
                sh = _shift_rows(q, b, SUBLANES, R)
                acc = sh if acc is None else acc + sh
            z_ref[r0:r0 + R, cols] = acc + cbb_ref[:, cols]
    for r0 in range(0, T, 16):
        rows = slice(r0, r0 + 16)
        z = z_ref[rows, :]
        mu = jnp.mean(z, axis=-1, keepdims=True)
        zc = z - mu
        var = jnp.mean(zc * zc, axis=-1, keepdims=True)
        y = zc * lax.rsqrt(var + EPS) * lng_ref[...] + lnb_ref[...]
        y = y * jax.nn.sigmoid(y)
        mg_ref[rows, :] = gate(1, rows) * y

    for r0 in range(0, T, 32):
        rows = slice(r0, r0 + 32)
        ext_a[HALO_A + r0:HALO_A + r0 + 32, :] = proj_ref[rows, C_CA:C_GLU_V] * proj_ref[rows, C_HA:C_BA]
    for r0 in range(0, T, 32):
        rows = slice(r0, r0 + 32)
        for c0 in range(0, D, 256):
            cols = slice(c0, c0 + 256)
            e = ext_a[r0:r0 + 32 + HALO_A, cols]
            conv = (e[HALO_A:] * caw_ref[2:3, cols]
                    + _shift_rows(e, 1, HALO_A, 32) * caw_ref[1:2, cols]
                    + _shift_rows(e, 2, HALO_A, 32) * caw_ref[0:1, cols])
            y_a = proj_ref[rows, C_BA + c0:C_BA + c0 + 256] * conv
            mg_ref[rows, cols] += gate(0, rows, cols) * y_a

    for r0 in range(0, T, 32):
        pos1 = j * T + r0 + 1 + lax.broadcasted_iota(jnp.int32, (32, 1), 0)
        for g, w in enumerate(POOL_WINDOWS):
            cols = slice(g * GROUP, (g + 1) * GROUP)
            e = ext_p[r0:r0 + 32 + HALO_P, cols]
            s = e
            stride = 1
            while stride < w:
                s = s + _roll_down(s, stride)
                stride *= 2
            cnt = jnp.minimum(pos1, w).astype(F32)
            tok = e[HALO_P:]
            pooled = s[HALO_P:] * (1.0 / cnt) - tok
            pooled_ref[r0:r0 + 32, cols] = pooled.astype(BF16)
    for g in range(N_GROUPS):
        cols = slice(g * GROUP, (g + 1) * GROUP)
        y_c = _dot(pooled_ref[:, cols], pw_ref[g]) * psc_ref[:, cols]
        for r0 in range(0, T, 64):
            rows = slice(r0, r0 + 64)
            mg_ref[rows, cols] += gate(2, rows, cols) * y_c[r0:r0 + 64]

    for h in range(N_HEADS):
        cols = slice(h * HEAD, (h + 1) * HEAD)
        qh = proj_ref[:, P_Q + h * HEAD:P_Q + (h + 1) * HEAD].astype(BF16)
        s = lax.dot_general(qh, kb_ref[0, 0, :, cols], (((1,), (1,)), ((), ())),
                            preferred_element_type=F32) * (HEAD ** -0.5)
        for r0 in range(0, T, 64):
            sc = s[r0:r0 + 64]
            m = jnp.max(sc, axis=-1, keepdims=True)
            e = jnp.exp(sc - m)
            p = e * (1.0 / jnp.sum(e, axis=-1, keepdims=True))
            p_ref[r0:r0 + 64, :] = p.astype(BF16)
        y_m = _dot(p_ref[...], vb_ref[0, 0, :, cols])
        for r0 in range(0, T, 64):
            rows = slice(r0, r0 + 64)
            mg_ref[rows, cols] += gate(3, rows, cols) * y_m[r0:r0 + 64]

    for r0 in range(0, T, 32):
        mgb_ref[r0:r0 + 32, :] = mg_ref[r0:r0 + 32, :].astype(BF16)
    x1_ref[0] = x_ref[0] + _dot(mgb_ref[...], w_o_ref[...])

    ta = ext_a[T:T + HALO_A, :]
    tb = ext_b[T:T + HALO_B, :]
    tp = ext_p[T:T + HALO_P, :]
    ta_ref[0] = ta
    tb_ref[0] = tb
    tp_ref[0] = tp
    ext_a[0:HALO_A, :] = ta
    ext_b[0:HALO_B, :] = tb
    ext_p[0:HALO_P, :] = tp


def _roll_down(v, s):
    return pltpu.roll(v, s, axis=0)


def _const_spec(shape):
    nd = len(shape)
    return pl.BlockSpec(shape, lambda b, j: (0,) * nd, pipeline_mode=pl.Buffered(1))


def _mix_call(l, x, kb, vb, w_in, w_o, pw_b, nmix, caw, cbw, cbb, lng, lnb, psc, gb):
    nb, seq, _ = x.shape
    T = T_MIX
    vec = lambda a: a.reshape(1, -1)
    kv_spec = pl.BlockSpec((1, 1, N_MEM, D), lambda b, j: (l, b, 0, 0))
    hbm = pl.BlockSpec(memory_space=pl.ANY)
    return pl.pallas_call(
        functools.partial(_mix_kernel, l=l),
        grid=(nb, seq // T),
        in_specs=[pl.BlockSpec((1, T, D), lambda b, j: (b, j, 0)),
                  kv_spec, kv_spec,
                  hbm, hbm, _const_spec((N_GROUPS, GROUP, GROUP)),
                  _const_spec((1, D)), _const_spec((KA, D)), _const_spec((KB, D)),
                  _const_spec((1, D)), _const_spec((1, D)), _const_spec((1, D)), _const_spec((1, D)),
                  _const_spec((1, 4 * D))],
        out_specs=[pl.BlockSpec((1, T, D), lambda b, j: (b, j, 0)),
                   pl.BlockSpec((1, HALO_A, D), lambda b, j: (b, 0, 0)),
                   pl.BlockSpec((1, HALO_B, D), lambda b, j: (b, 0, 0)),
                   pl.BlockSpec((1, HALO_P, D), lambda b, j: (b, 0, 0))],
        out_shape=[jax.ShapeDtypeStruct((nb, seq, D), F32),
                   jax.ShapeDtypeStruct((nb, HALO_A, D), F32),
                   jax.ShapeDtypeStruct((nb, HALO_B, D), F32),
                   jax.ShapeDtypeStruct((nb, HALO_P, D), F32)],
        scratch_shapes=[pltpu.VMEM((D, D_PROJ), BF16),
                        pltpu.VMEM((D, D), BF16)]
        + _stage_scratch()
        + [pltpu.VMEM((T, D), BF16),
           pltpu.VMEM((T, 2 * D), F32),
           pltpu.VMEM((T, 8 * D), F32),
           pltpu.VMEM((HALO_A + T, D), F32),
           pltpu.VMEM((HALO_B + T, D), F32),
           pltpu.VMEM((HALO_P + T, D), F32),
           pltpu.VMEM((T, D), F32),
           pltpu.VMEM((T, D), F32),
           pltpu.VMEM((T, D), BF16),
           pltpu.VMEM((T, D), BF16),
           pltpu.VMEM((T, N_MEM), BF16)],
        compiler_params=pltpu.CompilerParams(
            dimension_semantics=("arbitrary", "arbitrary"), vmem_limit_bytes=VMEM_LIMIT),
        name=f"mix_prompt_l{l}",
    )(x, kb, vb, w_in, w_o, pw_b, vec(nmix), caw, cbw, vec(cbb), vec(lng), vec(lnb), vec(psc),
      vec(gb))


def _ffn_kernel(*refs, l, tile, with_merge, final_norm, attn_bt):
    R = 32
    if with_merge:
        (x_ref, part_ref, g3_ref, ym_ref, w_o_hbm, nffn_ref, w1_hbm, w2_hbm, nfin_ref,
         o_ref, w1_ref, w2_ref, stage_ref, stage_sem, xn_ref, h_ref, w_o_ref, x1_ref) = refs
        jobs = [(w_o_hbm.at[l], w_o_ref)]
    elif attn_bt:
        (x1_ref, nffn_ref, w1_hbm, w2_hbm, nfin_ref, q_ref, k_ref, v_ref,
         o_ref, att_ref, w1_ref, w2_ref, stage_ref, stage_sem, xn_ref, h_ref) = refs
        jobs = []
    else:
        (x1_ref, nffn_ref, w1_hbm, w2_hbm, nfin_ref,
         o_ref, w1_ref, w2_ref, stage_ref, stage_sem, xn_ref, h_ref) = refs
        jobs = []
    jobs += [(w1_hbm.at[l], w1_ref), (w2_hbm.at[l], w2_ref)]

    @pl.when(pl.program_id(0) == 0)
    def _():
        _stage_weights(jobs, stage_ref, stage_sem)

    if attn_bt:
        _decode_attention(q_ref, k_ref, v_ref, att_ref, attn_bt)

    if with_merge:
        for r0 in range(0, tile, R):
            rows = slice(r0, r0 + R)
            xn_ref[rows, :] = (part_ref[rows, :] + g3_ref[rows, :] * ym_ref[rows, :]).astype(BF16)
        x1_ref[...] = x_ref[...] + _dot(xn_ref[...], w_o_ref[...])
    g_ffn = nffn_ref[...]
    for r0 in range(0, tile, R):
        rows = slice(r0, r0 + R)
        xn_ref[rows, :] = _rms_rows(x1_ref[rows, :], g_ffn).astype(BF16)
    for f0 in range(0, D_FF, D):
        h = jnp.maximum(_dot(xn_ref[...], w1_ref[:, f0:f0 + D]), 0.0)
        h_ref[:, f0:f0 + D] = (h * h).astype(BF16)
    y = x1_ref[...] + _dot(h_ref[...], w2_ref[...])
    if final_norm:
        y = _rms_rows(y, nfin_ref[...])
    o_ref[...] = y


def _ffn_call(l, x, nffn, w1, w2, nfin, *, final_norm, merge=None, attn=None, name):
    n = x.shape[0]
    tile = min(T_FFN, n)
    steps = n // tile
    row = pl.BlockSpec((tile, D), lambda i: (i, 0))
    const = lambda shape: pl.BlockSpec(shape, lambda i: (0,) * len(shape), pipeline_mode=pl.Buffered(1))
    hbm = pl.BlockSpec(memory_space=pl.ANY)
    args, specs = [x], [row]
    out_specs, out_shape, attn_bt = row, jax.ShapeDtypeStruct((n, D), F32), 0
    if merge is not None:
        part, g3, ym, w_o = merge
        args += [part, g3, ym, w_o]
        specs += [row, row, row, hbm]
    args += [nffn.reshape(1, D), w1, w2, nfin.reshape(1, D)]
    specs += [const((1, D)), hbm, hbm, const((1, D))]
    if attn is not None:
        q, cache_k, cache_v = attn
        ns = q.shape[0]
        attn_bt = ns // steps
        assert attn_bt * steps == ns
        kv = pl.BlockSpec((1, attn_bt, N_MEM, N_HEADS, HEAD), lambda i: (l, i, 0, 0, 0))
        qo = pl.BlockSpec((attn_bt, N_HEADS, HEAD), lambda i: (i, 0, 0))
        args += [q.reshape(ns, N_HEADS, HEAD), cache_k, cache_v]
        specs += [qo, kv, kv]
        out_specs = [row, qo]
        out_shape = [out_shape, jax.ShapeDtypeStruct((ns, N_HEADS, HEAD), F32)]
    return pl.pallas_call(
        functools.partial(_ffn_kernel, l=l, tile=tile, with_merge=merge is not None,
                          final_norm=final_norm, attn_bt=attn_bt),
        grid=(steps,),
        in_specs=specs,
        out_specs=out_specs,
        out_shape=out_shape,
        scratch_shapes=[pltpu.VMEM((D, D_FF), BF16), pltpu.VMEM((D_FF, D), BF16)]
        + _stage_scratch()
        + [pltpu.VMEM((tile, D), BF16), pltpu.VMEM((tile, D_FF), BF16)]
        + ([pltpu.VMEM((D, D), BF16), pltpu.VMEM((tile, D), F32)] if merge is not None else []),
        compiler_params=pltpu.CompilerParams(
            dimension_semantics=("arbitrary",), vmem_limit_bytes=VMEM_LIMIT),
        name=name,
    )(*args)


def _state_kernel(st_ref, nxt_ref, wt_ref, red_ref, sh_ref, *, kb):
    k = pl.program_id(1)
    acc = None
    for i in range(kb):
        row = st_ref[0, i]
        term = row * wt_ref[0, pl.ds(k * kb + i, 1), :]
        acc = term if acc is None else acc + term
        if i > 0:
            sh_ref[0, i - 1] = row
    sh_ref[0, kb - 1] = nxt_ref[0, 0]

    @pl.when(k == 0)
    def _():
        red_ref[0] = acc

    @pl.when(k > 0)
    def _():
        red_ref[0] += acc


def _state_call(state_t, wt, kb, name):
    _, k_len, n, _ = state_t.shape
    assert k_len % kb == 0
    blk = pl.BlockSpec((1, kb, n, D), lambda l, k: (l, k, 0, 0))
    return pl.pallas_call(
        functools.partial(_state_kernel, kb=kb),
        grid=(DEPTH, k_len // kb),
        in_specs=[blk,
                  pl.BlockSpec((1, 1, n, D), lambda l, k: (l, jnp.minimum((k + 1) * kb, k_len - 1), 0, 0)),
                  pl.BlockSpec((1, k_len, D), lambda l, k: (l, 0, 0))],
        out_specs=[pl.BlockSpec((1, n, D), lambda l, k: (l, 0, 0)), blk],
        out_shape=[jax.ShapeDtypeStruct((DEPTH, n, D), F32),
                   jax.ShapeDtypeStruct(state_t.shape, F32)],
        compiler_params=pltpu.CompilerParams(
            dimension_semantics=("arbitrary", "arbitrary"), vmem_limit_bytes=VMEM_LIMIT),
        name=name,
    )(state_t, state_t, wt)


def _mix_sample_kernel(x_ref, za_ref, zb_ref, sp_ref, w_in_hbm, pw_ref, nmix_ref, caw_ref, cbw_ref,
                       cbb_ref, lng_ref, lnb_ref, psc_ref, gb_ref, sha_in, shb_in, shp_in,
                       part_ref, q_ref, g3_ref, u_ref, glu_ref, pin_ref,
                       w_in_ref, stage_ref, stage_sem, xn_ref, proj_ref, pooled_ref, *, l, n):
    del sha_in, shb_in, shp_in
    _stage_weights([(w_in_hbm.at[l], w_in_ref)], stage_ref, stage_sem)
    R = 32
    g_mix = nmix_ref[...]
    for r0 in range(0, n, R):
        xn_ref[r0:r0 + R, :] = _rms_rows(x_ref[r0:r0 + R, :], g_mix).astype(BF16)
    for c0 in range(0, D_PROJ, D):
        proj_ref[:, c0:c0 + D] = _dot(xn_ref[...], w_in_ref[:, c0:c0 + D])

    def gate(i, rows, cols=slice(0, D)):
        c = slice(C_GATE + i * D + cols.start, C_GATE + i * D + cols.stop)
        bc = slice(i * D + cols.start, i * D + cols.stop)
        return jax.nn.sigmoid(proj_ref[rows, c] + gb_ref[:, bc])

    cnt = [float(min(PAST_LEN + 1, w)) for w in POOL_WINDOWS]
    for r0 in range(0, n, R):
        rows = slice(r0, r0 + R)
        u = proj_ref[rows, C_CA:C_GLU_V] * proj_ref[rows, C_HA:C_BA]
        u_ref[0, 0, rows, :] = u
        y_a = proj_ref[rows, C_BA:C_CA] * (za_ref[0, rows, :] + u * caw_ref[KA - 1:KA, :])
        merged = gate(0, rows) * y_a
        glu = proj_ref[rows, C_GLU_V:C_GLU_G] * jax.nn.sigmoid(proj_ref[rows, C_GLU_G:C_P])
        glu_ref[0, 0, rows, :] = glu
        z = zb_ref[0, rows, :] + glu * cbw_ref[KB - 1:KB, :] + cbb_ref[...]
        mu = jnp.mean(z, axis=-1, keepdims=True)
        zc = z - mu
        var = jnp.mean(zc * zc, axis=-1, keepdims=True)
        y = zc * lax.rsqrt(var + EPS) * lng_ref[...] + lnb_ref[...]
        y = y * jax.nn.sigmoid(y)
        merged = merged + gate(1, rows) * y
        part_ref[rows, :] = merged
        p_in = proj_ref[rows, C_P:C_Q]
        pin_ref[0, 0, rows, :] = p_in
        s = sp_ref[0, rows, :] + p_in
        for g in range(N_GROUPS):
            cols = slice(g * GROUP, (g + 1) * GROUP)
            pooled_ref[rows, cols] = (s[:, cols] * (1.0 / cnt[g]) - p_in[:, cols]).astype(BF16)
        q_ref[rows, :] = proj_ref[rows, C_Q:C_GATE]
        g3_ref[rows, :] = gate(3, rows)
    for g in range(N_GROUPS):
        cols = slice(g * GROUP, (g + 1) * GROUP)
        y_c = _dot(pooled_ref[:, cols], pw_ref[g]) * psc_ref[:, cols]
        for r0 in range(0, n, R):
            rows = slice(r0, r0 + R)
            part_ref[rows, cols] += gate(2, rows, cols) * y_c[r0:r0 + R]


def _mix_sample_call(l, x, za, zb, sp, w_in, pw_b, nmix, caw, cbw, cbb, lng, lnb, psc, gb,
                     sha, shb, shp):
    n = x.shape[0]
    vec = lambda a: a.reshape(1, -1)
    const = lambda shape: pl.BlockSpec(shape, lambda i: (0,) * len(shape), pipeline_mode=pl.Buffered(1))
    lay = pl.BlockSpec((1, n, D), lambda i: (l, 0, 0))
    row = pl.BlockSpec((n, D), lambda i: (0, 0))
    out = jax.ShapeDtypeStruct((n, D), F32)
    states = (sha, shb, shp)
    last_slot = lambda st: pl.BlockSpec((1, 1, n, D), lambda i: (l, st.shape[1] - 1, 0, 0))
    n_in = 14
    return pl.pallas_call(
        functools.partial(_mix_sample_kernel, l=l, n=n),
        grid=(1,),
        in_specs=[row, lay, lay, lay,
                  pl.BlockSpec(memory_space=pl.ANY), const((N_GROUPS, GROUP, GROUP)),
                  const((1, D)), const((KA, D)), const((KB, D)),
                  const((1, D)), const((1, D)), const((1, D)), const((1, D)), const((1, 4 * D))]
        + [pl.BlockSpec(memory_space=pl.ANY)] * 3,
        out_specs=[row] * 3 + [last_slot(st) for st in states],
        out_shape=[out] * 3 + [jax.ShapeDtypeStruct(st.shape, F32) for st in states],
        input_output_aliases={n_in: 3, n_in + 1: 4, n_in + 2: 5},
        scratch_shapes=[pltpu.VMEM((D, D_PROJ), BF16)] + _stage_scratch()
        + [pltpu.VMEM((n, D), BF16), pltpu.VMEM((n, D_PROJ), F32), pltpu.VMEM((n, D), BF16)],
        compiler_params=pltpu.CompilerParams(
            dimension_semantics=("arbitrary",), vmem_limit_bytes=VMEM_LIMIT),
        name=f"mix_sample_l{l}",
    )(x, za, zb, sp, w_in, pw_b, vec(nmix), caw, cbw, vec(cbb), vec(lng), vec(lnb), vec(psc), vec(gb),
      sha, shb, shp)


def _decode_attention(q_ref, k_ref, v_ref, o_ref, bt):
    half = N_MEM // 2
    pair = lambda ref, i: jnp.concatenate([ref[0, i, 0:half], ref[0, i, half:N_MEM]], axis=1)
    both = lambda a: jnp.concatenate([a, a], axis=1)
    for i in range(bt):
        q = q_ref[i]
        q8 = jnp.concatenate([q, q], axis=0)
        s = jnp.sum(pair(k_ref, i) * q8[None], axis=-1, keepdims=True) * (HEAD ** -0.5)
        m8 = jnp.max(s, axis=0, keepdims=True)
        m = both(jnp.maximum(m8[:, 0:N_HEADS], m8[:, N_HEADS:]))
        e = jnp.exp(s - m)
        d8 = jnp.sum(e, axis=0, keepdims=True)
        p = e * (1.0 / both(d8[:, 0:N_HEADS] + d8[:, N_HEADS:]))
        y8 = jnp.sum(p * pair(v_ref, i), axis=0)
        o_ref[i] = y8[0:N_HEADS] + y8[N_HEADS:]


def kernel(x_prompt, x_sample, mem_prompt, cache_mem_k, cache_mem_v, state_conv_a, state_conv_b,
           state_pool, norm_mix, norm_mem, w_kv, w_in, conv_a_w, conv_b_w, conv_b_bias, ln_b_gain,
           ln_b_bias, pool_w, pool_scale, gate_bias, w_o, norm_ffn, w_ff1, w_ff2, norm_final):
    nb, seq, _ = x_prompt.shape
    ns = x_sample.shape[0]

    w_kv_b = w_kv.astype(BF16)
    pw_b = pool_w.astype(BF16)

    mem_k_prompt, mem_v_prompt, kb, vb = _kv_call(mem_prompt, norm_mem, w_kv_b)

    pool_mask = jnp.concatenate(
        [jnp.broadcast_to((jnp.arange(POOL_BUF) >= POOL_BUF + 1 - w).astype(F32)[:, None], (POOL_BUF, GROUP))
         for w in POOL_WINDOWS], axis=1)
    time_major = lambda a: jnp.transpose(a, (0, 2, 1, 3))
    za, sha = _state_call(time_major(state_conv_a), conv_a_w[:, :KA - 1], 2, "state_conv_a")
    zb, shb = _state_call(time_major(state_conv_b), conv_b_w[:, :KB - 1], 6, "state_conv_b")
    sp, shp = _state_call(time_major(state_pool), jnp.broadcast_to(pool_mask, (DEPTH, POOL_BUF, D)),
                          5, "state_pool")
    x = x_prompt
    xs = x_sample.reshape(ns, D)
    tails_a, tails_b, tails_p = [], [], []
    for l in range(DEPTH):
        layer = (norm_mix[l], conv_a_w[l], conv_b_w[l], conv_b_bias[l], ln_b_gain[l], ln_b_bias[l],
                 pool_scale[l], gate_bias[l])
        x, ta, tb, tp = _mix_call(l, x, kb, vb, w_in, w_o, pw_b[l], *layer)
        part, q, g3, sha, shb, shp = _mix_sample_call(l, xs, za, zb, sp, w_in, pw_b[l], *layer,
                                                      sha, shb, shp)
        x, ym = _ffn_call(l, x.reshape(nb * seq, D), norm_ffn[l], w_ff1, w_ff2, norm_final,
                          final_norm=(l == DEPTH - 1), attn=(q, cache_mem_k, cache_mem_v),
                          name=f"ffn_prompt_l{l}")
        x = x.reshape(nb, seq, D)
        xs = _ffn_call(l, xs, norm_ffn[l], w_ff1, w_ff2, norm_final, final_norm=(l == DEPTH - 1),
                       merge=(part, g3, ym.reshape(ns, D), w_o), name=f"ffn_sample_l{l}")
        tails_a.append(ta[:, HALO_A - (KA - 1):])
        tails_b.append(tb[:, HALO_B - (KB - 1):])
        tails_p.append(tp[:, HALO_P - POOL_BUF:])
    y_prompt = x
    conv_a_prompt = jnp.stack(tails_a)
    conv_b_prompt = jnp.stack(tails_b)
    pool_prompt = jnp.stack(tails_p)
    y_sample = xs.reshape(ns, 1, D)
    conv_a_sample = time_major(sha)
    conv_b_sample = time_major(shb)
    pool_sample = time_major(shp)

    return (y_prompt, y_sample, mem_k_prompt, mem_v_prompt, conv_a_prompt, conv_b_prompt,
            pool_prompt, conv_a_sample, conv_b_sample, pool_sample)
```

```python
import functools

import jax
import jax.numpy as jnp
from jax import lax
from jax.experimental import pallas as pl
from jax.experimental.pallas import tpu as pltpu

D = 1024
DEPTH = 2
N_MEM = 256
N_HEADS = 4
HEAD = D // N_HEADS
N_GROUPS = 4
GROUP = D // N_GROUPS
POOL_WINDOWS = (2, 4, 8, 16)
POOL_BUF = 15
KA = 3
KB = 31
D_FF = 4 * D
D_PROJ = 11 * D
PAST_LEN = 16384
EPS = 1e-6

C_HA, C_BA, C_CA, C_GLU_V, C_GLU_G, C_P, C_Q, C_GATE = (
    0, D, 2 * D, 3 * D, 4 * D, 5 * D, 6 * D, 7 * D)

SUBLANES = 8
LANES = 128
VMEM_LIMIT = 60 * 1024 * 1024

T_MIX = 256
T_FFN = 512
HALO_A, HALO_B, HALO_P = 8, 32, 16
STAGE_ROWS, STAGE_COLS = 256, 1024
STAGE_SLOTS = 4

BF16 = jnp.bfloat16
F32 = jnp.float32


def _dot(a, b):
    return jnp.dot(a, b, preferred_element_type=F32)


def _rms_rows(x, g):
    ms = jnp.mean(x * x, axis=-1, keepdims=True)
    return x * lax.rsqrt(ms + EPS) * g


def _stage_weights(jobs, stage_ref, sem_ref):
    chunks = [(src, dst, r0, c0) for src, dst in jobs
              for r0 in range(0, src.shape[0], STAGE_ROWS)
              for c0 in range(0, src.shape[1], STAGE_COLS)]
    ahead = STAGE_SLOTS - 1

    def copy(i):
        src, _, r0, c0 = chunks[i]
        return pltpu.make_async_copy(src.at[pl.ds(r0, STAGE_ROWS), pl.ds(c0, STAGE_COLS)],
                                     stage_ref.at[i % STAGE_SLOTS], sem_ref.at[i % STAGE_SLOTS])

    for i in range(min(ahead, len(chunks))):
        copy(i).start()
    for i, (_, dst, r0, c0) in enumerate(chunks):
        if i + ahead < len(chunks):
            copy(i + ahead).start()
        copy(i).wait()
        dst[r0:r0 + STAGE_ROWS, c0:c0 + STAGE_COLS] = stage_ref[i % STAGE_SLOTS].astype(BF16)


def _stage_scratch():
    return [pltpu.VMEM((STAGE_SLOTS, STAGE_ROWS, STAGE_COLS), F32),
            pltpu.SemaphoreType.DMA((STAGE_SLOTS,))]


def _shift_rows(v, s, halo, rows):
    return v[halo - s:halo - s + rows]


def _kv_kernel(mem_ref, g_ref, w_ref, k_ref, v_ref, kb_ref, vb_ref, *, bb):
    mem = mem_ref[...].reshape(bb * N_MEM, D)
    memn = _rms_rows(mem, g_ref[0]).astype(BF16)
    kv = _dot(memn, w_ref[0])
    for r in range(bb):
        k = kv[r * N_MEM:(r + 1) * N_MEM, :D]
        v = kv[r * N_MEM:(r + 1) * N_MEM, D:]
        k_ref[0, r] = k.reshape(N_MEM, N_HEADS, HEAD)
        v_ref[0, r] = v.reshape(N_MEM, N_HEADS, HEAD)
        kb_ref[0, r] = k.astype(BF16)
        vb_ref[0, r] = v.astype(BF16)


def _kv_call(mem, norm_mem, w_kv_b):
    nb = mem.shape[0]
    bb = 2
    out_f = jax.ShapeDtypeStruct((DEPTH, nb, N_MEM, N_HEADS, HEAD), F32)
    out_b = jax.ShapeDtypeStruct((DEPTH, nb, N_MEM, D), BF16)
    blk = pl.BlockSpec((1, bb, N_MEM, D), lambda l, b: (l, b, 0, 0))
    blk_f = pl.BlockSpec((1, bb, N_MEM, N_HEADS, HEAD), lambda l, b: (l, b, 0, 0, 0))
    return pl.pallas_call(
        functools.partial(_kv_kernel, bb=bb),
        grid=(DEPTH, nb // bb),
        in_specs=[pl.BlockSpec((bb, N_MEM, D), lambda l, b: (b, 0, 0)),
                  pl.BlockSpec((1, 1, D), lambda l, b: (l, 0, 0)),
                  pl.BlockSpec((1, D, 2 * D), lambda l, b: (l, 0, 0))],
        out_specs=[blk_f, blk_f, blk, blk],
        out_shape=[out_f, out_f, out_b, out_b],
        compiler_params=pltpu.CompilerParams(
            dimension_semantics=("arbitrary", "arbitrary"), vmem_limit_bytes=VMEM_LIMIT),
        name="kv_proj",
    )(mem, norm_mem.reshape(DEPTH, 1, D), w_kv_b)


def _mix_kernel(x_ref, kb_ref, vb_ref, w_in_hbm, w_o_hbm, pw_ref, nmix_ref, caw_ref,
                cbw_ref, cbb_ref, lng_ref, lnb_ref, psc_ref, gb_ref,
                x1_ref, ta_ref, tb_ref, tp_ref,
                w_in_ref, w_o_ref, stage_ref, stage_sem,
                xn_ref, pb_ref, proj_ref, ext_a, ext_b, ext_p, z_ref, mg_ref, mgb_ref, pooled_ref,
                p_ref, *, l):
    T = T_MIX
    j = pl.program_id(1)

    @pl.when((pl.program_id(0) == 0) & (j == 0))
    def _():
        _stage_weights([(w_in_hbm.at[l], w_in_ref), (w_o_hbm.at[l], w_o_ref)], stage_ref, stage_sem)

    @pl.when(j == 0)
    def _():
        ext_a[0:HALO_A, :] = jnp.zeros((HALO_A, D), F32)
        ext_b[0:HALO_B, :] = jnp.zeros((HALO_B, D), F32)
        ext_p[0:HALO_P, :] = jnp.zeros((HALO_P, D), F32)

    g_mix = nmix_ref[...]
    for r0 in range(0, T, 32):
        xn_ref[r0:r0 + 32, :] = _rms_rows(x_ref[0, r0:r0 + 32, :], g_mix).astype(BF16)

    pb_ref[...] = _dot(xn_ref[...], w_in_ref[:, C_GLU_V:C_P])
    P_Q, P_GATE = 3 * D, 4 * D
    proj_ref[:, 0:P_Q] = _dot(xn_ref[...], w_in_ref[:, C_HA:C_GLU_V])
    ext_p[HALO_P:HALO_P + T, :] = _dot(xn_ref[...], w_in_ref[:, C_P:C_Q])
    proj_ref[:, P_Q:P_GATE] = _dot(xn_ref[...], w_in_ref[:, C_Q:C_GATE])
    proj_ref[:, P_GATE:P_GATE + 4 * D] = _dot(xn_ref[...], w_in_ref[:, C_GATE:D_PROJ])

    def gate(i, rows, cols=slice(0, D)):
        c = slice(P_GATE + i * D + cols.start, P_GATE + i * D + cols.stop)
        bc = slice(i * D + cols.start, i * D + cols.stop)
        return jax.nn.sigmoid(proj_ref[rows, c] + gb_ref[:, bc])

    for r0 in range(0, T, 32):
        rows = slice(r0, r0 + 32)
        glu = pb_ref[rows, 0:D] * jax.nn.sigmoid(pb_ref[rows, D:2 * D])
        ext_b[HALO_B + r0:HALO_B + r0 + 32, :] = glu

    R = 128
    for r0 in range(0, T, R):
        for c0 in range(0, D, LANES):
            cols = slice(c0, c0 + LANES)
            acc = None
            for b in range(SUBLANES):
                q = None
                for a in range(4):
                    d = SUBLANES * a + b
                    if d > KB - 1:
                        continue
                    base = HALO_B + r0 - SUBLANES - SUBLANES * a
                    term = ext_b[base:base + R + SUBLANES, cols] * cbw_ref[KB - 1 - d:KB - d, cols]
                    q = term if q is None else q + term
                sh = _shift_rows(q, b, SUBLANES, R)
                acc = sh if acc is None else acc + sh
            z_ref[r0:r0 + R, cols] = acc + cbb_ref[:, cols]
    for r0 in range(0, T, 16):
        rows = slice(r0, r0 + 16)
        z = z_ref[rows, :]
        mu = jnp.mean(z, axis=-1, keepdims=True)
        zc = z - mu
        var = jnp.mean(zc * zc, axis=-1, keepdims=True)
        y = zc * lax.rsqrt(var + EPS) * lng_ref[...] + lnb_ref[...]
        y = y * jax.nn.sigmoid(y)
        mg_ref[rows, :] = gate(1, rows) * y

    for r0 in range(0, T, 32):
        rows = slice(r0, r0 + 32)
        ext_a[HALO_A + r0:HALO_A + r0 + 32, :] = proj_ref[rows, C_CA:C_GLU_V] * proj_ref[rows, C_HA:C_BA]
    for r0 in range(0, T, 32):
        rows = slice(r0, r0 + 32)
        for c0 in range(0, D, 256):
            cols = slice(c0, c0 + 256)
            e = ext_a[r0:r0 + 32 + HALO_A, cols]
            conv = (e[HALO_A:] * caw_ref[2:3, cols]
                    + _shift_rows(e, 1, HALO_A, 32) * caw_ref[1:2, cols]
                    + _shift_rows(e, 2, HALO_A, 32) * caw_ref[0:1, cols])
            y_a = proj_ref[rows, C_BA + c0:C_BA + c0 + 256] * conv
            mg_ref[rows, cols] += gate(0, rows, cols) * y_a

    for r0 in range(0, T, 32):
        pos1 = j * T + r0 + 1 + lax.broadcasted_iota(jnp.int32, (32, 1), 0)
        for g, w in enumerate(POOL_WINDOWS):
            cols = slice(g * GROUP, (g + 1) * GROUP)
            e = ext_p[r0:r0 + 32 + HALO_P, cols]
            s = e
            stride = 1
            while stride < w:
                s = s + _roll_down(s, stride)
                stride *= 2
            cnt = jnp.minimum(pos1, w).astype(F32)
            tok = e[HALO_P:]
            pooled = s[HALO_P:] * (1.0 / cnt) - tok
            pooled_ref[r0:r0 + 32, cols] = pooled.astype(BF16)
    for g in range(N_GROUPS):
        cols = slice(g * GROUP, (g + 1) * GROUP)
        y_c = _dot(pooled_ref[:, cols], pw_ref[g]) * psc_ref[:, cols]
        for r0 in range(0, T, 64):
            rows = slice(r0, r0 + 64)
            mg_ref[rows, cols] += gate(2, rows, cols) * y_c[r0:r0 + 64]

    for h in range(N_HEADS):
        cols = slice(h * HEAD, (h + 1) * HEAD)
        qh = proj_ref[:, P_Q + h * HEAD:P_Q + (h + 1) * HEAD].astype(BF16)
        s = lax.dot_general(qh, kb_ref[0, 0, :, cols], (((1,), (1,)), ((), ())),
                            preferred_element_type=F32) * (HEAD ** -0.5)
        for r0 in range(0, T, 64):
            sc = s[r0:r0 + 64]
            m = jnp.max(sc, axis=-1, keepdims=True)
            e = jnp.exp(sc - m)
            p = e * (1.0 / jnp.sum(e, axis=-1, keepdims=True))
            p_ref[r0:r0 + 64, :] = p.astype(BF16)
        y_m = _dot(p_ref[...], vb_ref[0, 0, :, cols])
        for r0 in range(0, T, 64):
            rows = slice(r0, r0 + 64)
            mg_ref[rows, cols] += gate(3, rows, cols) * y_m[r0:r0 + 64]

    for r0 in range(0, T, 32):
        mgb_ref[r0:r0 + 32, :] = mg_ref[r0:r0 + 32, :].astype(BF16)
    x1_ref[0] = x_ref[0] + _dot(mgb_ref[...], w_o_ref[...])

    ta = ext_a[T:T + HALO_A, :]
    tb = ext_b[T:T + HALO_B, :]
    tp = ext_p[T:T + HALO_P, :]
    ta_ref[0] = ta
    tb_ref[0] = tb
    tp_ref[0] = tp
    ext_a[0:HALO_A, :] = ta
    ext_b[0:HALO_B, :] = tb
    ext_p[0:HALO_P, :] = tp


def _roll_down(v, s):
    return pltpu.roll(v, s, axis=0)


def _const_spec(shape):
    nd = len(shape)
    return pl.BlockSpec(shape, lambda b, j: (0,) * nd, pipeline_mode=pl.Buffered(1))


def _mix_call(l, x, kb, vb, w_in, w_o, pw_b, nmix, caw, cbw, cbb, lng, lnb, psc, gb):
    nb, seq, _ = x.shape
    T = T_MIX
    vec = lambda a: a.reshape(1, -1)
    kv_spec = pl.BlockSpec((1, 1, N_MEM, D), lambda b, j: (l, b, 0, 0))
    hbm = pl.BlockSpec(memory_space=pl.ANY)
    return pl.pallas_call(
        functools.partial(_mix_kernel, l=l),
        grid=(nb, seq // T),
        in_specs=[pl.BlockSpec((1, T, D), lambda b, j: (b, j, 0)),
                  kv_spec, kv_spec,
                  hbm, hbm, _const_spec((N_GROUPS, GROUP, GROUP)),
                  _const_spec((1, D)), _const_spec((KA, D)), _const_spec((KB, D)),
                  _const_spec((1, D)), _const_spec((1, D)), _const_spec((1, D)), _const_spec((1, D)),
                  _const_spec((1, 4 * D))],
        out_specs=[pl.BlockSpec((1, T, D), lambda b, j: (b, j, 0)),
                   pl.BlockSpec((1, HALO_A, D), lambda b, j: (b, 0, 0)),
                   pl.BlockSpec((1, HALO_B, D), lambda b, j: (b, 0, 0)),
                   pl.BlockSpec((1, HALO_P, D), lambda b, j: (b, 0, 0))],
        out_shape=[jax.ShapeDtypeStruct((nb, seq, D), F32),
                   jax.ShapeDtypeStruct((nb, HALO_A, D), F32),
                   jax.ShapeDtypeStruct((nb, HALO_B, D), F32),
                   jax.ShapeDtypeStruct((nb, HALO_P, D), F32)],
        scratch_shapes=[pltpu.VMEM((D, D_PROJ), BF16),
                        pltpu.VMEM((D, D), BF16)]
        + _stage_scratch()
        + [pltpu.VMEM((T, D), BF16),
           pltpu.VMEM((T, 2 * D), F32),
           pltpu.VMEM((T, 8 * D), F32),
           pltpu.VMEM((HALO_A + T, D), F32),
           pltpu.VMEM((HALO_B + T, D), F32),
           pltpu.VMEM((HALO_P + T, D), F32),
           pltpu.VMEM((T, D), F32),
           pltpu.VMEM((T, D), F32),
           pltpu.VMEM((T, D), BF16),
           pltpu.VMEM((T, D), BF16),
           pltpu.VMEM((T, N_MEM), BF16)],
        compiler_params=pltpu.CompilerParams(
            dimension_semantics=("arbitrary", "arbitrary"), vmem_limit_bytes=VMEM_LIMIT),
        name=f"mix_prompt_l{l}",
    )(x, kb, vb, w_in, w_o, pw_b, vec(nmix), caw, cbw, vec(cbb), vec(lng), vec(lnb), vec(psc),
      vec(gb))


def _ffn_kernel(*refs, l, tile, with_merge, final_norm, attn_bt):
    R = 32
    if with_merge:
        (x_ref, part_ref, g3_ref, ym_ref, w_o_hbm, nffn_ref, w1_hbm, w2_hbm, nfin_ref,
         o_ref, w1_ref, w2_ref, stage_ref, stage_sem, xn_ref, h_ref, w_o_ref, x1_ref) = refs
        jobs = [(w_o_hbm.at[l], w_o_ref)]
    elif attn_bt:
        (x1_ref, nffn_ref, w1_hbm, w2_hbm, nfin_ref, q_ref, k_ref, v_ref,
         o_ref, att_ref, w1_ref, w2_ref, stage_ref, stage_sem, xn_ref, h_ref) = refs
        jobs = []
    else:
        (x1_ref, nffn_ref, w1_hbm, w2_hbm, nfin_ref,
         o_ref, w1_ref, w2_ref, stage_ref, stage_sem, xn_ref, h_ref) = refs
        jobs = []
    jobs += [(w1_hbm.at[l], w1_ref), (w2_hbm.at[l], w2_ref)]

    @pl.when(pl.program_id(0) == 0)
    def _():
        _stage_weights(jobs, stage_ref, stage_sem)

    def exact_zero(v):
        bits = pltpu.bitcast(v.astype(F32), jnp.int32)
        z = lax.shift_right_logical(lax.shift_right_logical(bits, 16), 16).astype(F32)
        return jnp.sum(z.reshape(-1, HEAD), axis=0, keepdims=True)

    if with_merge:
        for r0 in range(0, tile, R):
            rows = slice(r0, r0 + R)
            xn_ref[rows, :] = (part_ref[rows, :] + g3_ref[rows, :] * ym_ref[rows, :]).astype(BF16)
        x1_ref[...] = x_ref[...] + _dot(xn_ref[...], w_o_ref[...])
    g_ffn = nffn_ref[...]
    for r0 in range(0, tile, R):
        rows = slice(r0, r0 + R)
        xn_ref[rows, :] = _rms_rows(x1_ref[rows, :], g_ffn).astype(BF16)
    for f0 in range(0, D_FF, D):
        h = jnp.maximum(_dot(xn_ref[...], w1_ref[:, f0:f0 + D]), 0.0)
        h_ref[:, f0:f0 + D] = (h * h).astype(BF16)
        if attn_bt and f0 == 0:
            _decode_attention(q_ref, k_ref, v_ref, att_ref, attn_bt, exact_zero(h_ref[0:16, 0:HEAD]))
    if attn_bt:
        zero = exact_zero(att_ref[...])
        h_ref[0:16, 0:HEAD] = (h_ref[0:16, 0:HEAD].astype(F32) + zero).astype(BF16)
    y = x1_ref[...] + _dot(h_ref[...], w2_ref[...])
    if final_norm:
        y = _rms_rows(y, nfin_ref[...])
    o_ref[...] = y


def _ffn_call(l, x, nffn, w1, w2, nfin, *, final_norm, merge=None, attn=None, name):
    n = x.shape[0]
    tile = min(T_FFN, n)
    steps = n // tile
    row = pl.BlockSpec((tile, D), lambda i: (i, 0))
    const = lambda shape: pl.BlockSpec(shape, lambda i: (0,) * len(shape), pipeline_mode=pl.Buffered(1))
    hbm = pl.BlockSpec(memory_space=pl.ANY)
    args, specs = [x], [row]
    out_specs, out_shape, attn_bt = row, jax.ShapeDtypeStruct((n, D), F32), 0
    if merge is not None:
        part, g3, ym, w_o = merge
        args += [part, g3, ym, w_o]
        specs += [row, row, row, hbm]
    args += [nffn.reshape(1, D), w1, w2, nfin.reshape(1, D)]
    specs += [const((1, D)), hbm, hbm, const((1, D))]
    if attn is not None:
        q, cache_k, cache_v = attn
        ns = q.shape[0]
        attn_bt = ns // steps
        assert attn_bt * steps == ns
        kv = pl.BlockSpec((1, attn_bt, N_MEM, N_HEADS, HEAD), lambda i: (l, i, 0, 0, 0))
        qo = pl.BlockSpec((attn_bt, N_HEADS, HEAD), lambda i: (i, 0, 0))
        args += [q.reshape(ns, N_HEADS, HEAD), cache_k, cache_v]
        specs += [qo, kv, kv]
        out_specs = [row, qo]
        out_shape = [out_shape, jax.ShapeDtypeStruct((ns, N_HEADS, HEAD), F32)]
    return pl.pallas_call(
        functools.partial(_ffn_kernel, l=l, tile=tile, with_merge=merge is not None,
                          final_norm=final_norm, attn_bt=attn_bt),
        grid=(steps,),
        in_specs=specs,
        out_specs=out_specs,
        out_shape=out_shape,
        scratch_shapes=[pltpu.VMEM((D, D_FF), BF16), pltpu.VMEM((D_FF, D), BF16)]
        + _stage_scratch()
        + [pltpu.VMEM((tile, D), BF16), pltpu.VMEM((tile, D_FF), BF16)]
        + ([pltpu.VMEM((D, D), BF16), pltpu.VMEM((tile, D), F32)] if merge is not None else []),
        compiler_params=pltpu.CompilerParams(
            dimension_semantics=("arbitrary",), vmem_limit_bytes=VMEM_LIMIT),
        name=name,
    )(*args)


def _state_kernel(st_ref, nxt_ref, wt_ref, red_ref, sh_ref, *, kb):
    k = pl.program_id(1)
    acc = None
    for i in range(kb):
        row = st_ref[0, i]
        term = row * wt_ref[0, pl.ds(k * kb + i, 1), :]
        acc = term if acc is None else acc + term
        if i > 0:
            sh_ref[0, i - 1] = row
    sh_ref[0, kb - 1] = nxt_ref[0, 0]

    @pl.when(k == 0)
    def _():
        red_ref[0] = acc

    @pl.when(k > 0)
    def _():
        red_ref[0] += acc


def _state_call(state_t, wt, kb, name):
    _, k_len, n, _ = state_t.shape
    assert k_len % kb == 0
    blk = pl.BlockSpec((1, kb, n, D), lambda l, k: (l, k, 0, 0))
    return pl.pallas_call(
        functools.partial(_state_kernel, kb=kb),
        grid=(DEPTH, k_len // kb),
        in_specs=[blk,
                  pl.BlockSpec((1, 1, n, D), lambda l, k: (l, jnp.minimum((k + 1) * kb, k_len - 1), 0, 0)),
                  pl.BlockSpec((1, k_len, D), lambda l, k: (l, 0, 0))],
        out_specs=[pl.BlockSpec((1, n, D), lambda l, k: (l, 0, 0)), blk],
        out_shape=[jax.ShapeDtypeStruct((DEPTH, n, D), F32),
                   jax.ShapeDtypeStruct(state_t.shape, F32)],
        compiler_params=pltpu.CompilerParams(
            dimension_semantics=("arbitrary", "arbitrary"), vmem_limit_bytes=VMEM_LIMIT),
        name=name,
    )(state_t, state_t, wt)


def _mix_sample_kernel(x_ref, za_ref, zb_ref, sp_ref, w_in_hbm, pw_ref, nmix_ref, caw_ref, cbw_ref,
                       cbb_ref, lng_ref, lnb_ref, psc_ref, gb_ref, sha_in, shb_in, shp_in,
                       part_ref, q_ref, g3_ref, u_ref, glu_ref, pin_ref,
                       w_in_ref, stage_ref, stage_sem, xn_ref, proj_ref, pooled_ref, *, l, n):
    del sha_in, shb_in, shp_in
    _stage_weights([(w_in_hbm.at[l], w_in_ref)], stage_ref, stage_sem)
    R = 32
    g_mix = nmix_ref[...]
    for r0 in range(0, n, R):
        xn_ref[r0:r0 + R, :] = _rms_rows(x_ref[r0:r0 + R, :], g_mix).astype(BF16)
    for c0 in range(0, D_PROJ, D):
        proj_ref[:, c0:c0 + D] = _dot(xn_ref[...], w_in_ref[:, c0:c0 + D])

    def gate(i, rows, cols=slice(0, D)):
        c = slice(C_GATE + i * D + cols.start, C_GATE + i * D + cols.stop)
        bc = slice(i * D + cols.start, i * D + cols.stop)
        return jax.nn.sigmoid(proj_ref[rows, c] + gb_ref[:, bc])

    cnt = [float(min(PAST_LEN + 1, w)) for w in POOL_WINDOWS]
    for r0 in range(0, n, R):
        rows = slice(r0, r0 + R)
        u = proj_ref[rows, C_CA:C_GLU_V] * proj_ref[rows, C_HA:C_BA]
        u_ref[0, 0, rows, :] = u
        y_a = proj_ref[rows, C_BA:C_CA] * (za_ref[0, rows, :] + u * caw_ref[KA - 1:KA, :])
        merged = gate(0, rows) * y_a
        glu = proj_ref[rows, C_GLU_V:C_GLU_G] * jax.nn.sigmoid(proj_ref[rows, C_GLU_G:C_P])
        glu_ref[0, 0, rows, :] = glu
        z = zb_ref[0, rows, :] + glu * cbw_ref[KB - 1:KB, :] + cbb_ref[...]
        mu = jnp.mean(z, axis=-1, keepdims=True)
        zc = z - mu
        var = jnp.mean(zc * zc, axis=-1, keepdims=True)
        y = zc * lax.rsqrt(var + EPS) * lng_ref[...] + lnb_ref[...]
        y = y * jax.nn.sigmoid(y)
        merged = merged + gate(1, rows) * y
        part_ref[rows, :] = merged
        p_in = proj_ref[rows, C_P:C_Q]
        pin_ref[0, 0, rows, :] = p_in
        s = sp_ref[0, rows, :] + p_in
        for g in range(N_GROUPS):
            cols = slice(g * GROUP, (g + 1) * GROUP)
            pooled_ref[rows, cols] = (s[:, cols] * (1.0 / cnt[g]) - p_in[:, cols]).astype(BF16)
        q_ref[rows, :] = proj_ref[rows, C_Q:C_GATE]
        g3_ref[rows, :] = gate(3, rows)
    for g in range(N_GROUPS):
        cols = slice(g * GROUP, (g + 1) * GROUP)
        y_c = _dot(pooled_ref[:, cols], pw_ref[g]) * psc_ref[:, cols]
        for r0 in range(0, n, R):
            rows = slice(r0, r0 + R)
            part_ref[rows, cols] += gate(2, rows, cols) * y_c[r0:r0 + R]


def _mix_sample_call(l, x, za, zb, sp, w_in, pw_b, nmix, caw, cbw, cbb, lng, lnb, psc, gb,
                     sha, shb, shp):
    n = x.shape[0]
    vec = lambda a: a.reshape(1, -1)
    const = lambda shape: pl.BlockSpec(shape, lambda i: (0,) * len(shape), pipeline_mode=pl.Buffered(1))
    lay = pl.BlockSpec((1, n, D), lambda i: (l, 0, 0))
    row = pl.BlockSpec((n, D), lambda i: (0, 0))
    out = jax.ShapeDtypeStruct((n, D), F32)
    states = (sha, shb, shp)
    last_slot = lambda st: pl.BlockSpec((1, 1, n, D), lambda i: (l, st.shape[1] - 1, 0, 0))
    n_in = 14
    return pl.pallas_call(
        functools.partial(_mix_sample_kernel, l=l, n=n),
        grid=(1,),
        in_specs=[row, lay, lay, lay,
                  pl.BlockSpec(memory_space=pl.ANY), const((N_GROUPS, GROUP, GROUP)),
                  const((1, D)), const((KA, D)), const((KB, D)),
                  const((1, D)), const((1, D)), const((1, D)), const((1, D)), const((1, 4 * D))]
        + [pl.BlockSpec(memory_space=pl.ANY)] * 3,
        out_specs=[row] * 3 + [last_slot(st) for st in states],
        out_shape=[out] * 3 + [jax.ShapeDtypeStruct(st.shape, F32) for st in states],
        input_output_aliases={n_in: 3, n_in + 1: 4, n_in + 2: 5},
        scratch_shapes=[pltpu.VMEM((D, D_PROJ), BF16)] + _stage_scratch()
        + [pltpu.VMEM((n, D), BF16), pltpu.VMEM((n, D_PROJ), F32), pltpu.VMEM((n, D), BF16)],
        compiler_params=pltpu.CompilerParams(
            dimension_semantics=("arbitrary",), vmem_limit_bytes=VMEM_LIMIT),
        name=f"mix_sample_l{l}",
    )(x, za, zb, sp, w_in, pw_b, vec(nmix), caw, cbw, vec(cbb), vec(lng), vec(lnb), vec(psc), vec(gb),
      sha, shb, shp)


def _decode_attention(q_ref, k_ref, v_ref, o_ref, bt, q_zero):
    half = N_MEM // 2
    pair = lambda ref, i: jnp.concatenate([ref[0, i, 0:half], ref[0, i, half:N_MEM]], axis=1)
    both = lambda a: jnp.concatenate([a, a], axis=1)
    for i in range(bt):
        q = q_ref[i] + q_zero
        q8 = jnp.concatenate([q, q], axis=0)
        s = jnp.sum(pair(k_ref, i) * q8[None], axis=-1, keepdims=True) * (HEAD ** -0.5)
        m8 = jnp.max(s, axis=0, keepdims=True)
        m = both(jnp.maximum(m8[:, 0:N_HEADS], m8[:, N_HEADS:]))
        e = jnp.exp(s - m)
        d8 = jnp.sum(e, axis=0, keepdims=True)
        p = e * (1.0 / both(d8[:, 0:N_HEADS] + d8[:, N_HEADS:]))
        y8 = jnp.sum(p * pair(v_ref, i), axis=0)
        o_ref[i] = y8[0:N_HEADS] + y8[N_HEADS:]


def kernel(x_prompt, x_sample, mem_prompt, cache_mem_k, cache_mem_v, state_conv_a, state_conv_b,
           state_pool, norm_mix, norm_mem, w_kv, w_in, conv_a_w, conv_b_w, conv_b_bias, ln_b_gain,
           ln_b_bias, pool_w, pool_scale, gate_bias, w_o, norm_ffn, w_ff1, w_ff2, norm_final):
    nb, seq, _ = x_prompt.shape
    ns = x_sample.shape[0]

    w_kv_b = w_kv.astype(BF16)
    pw_b = pool_w.astype(BF16)

    mem_k_prompt, mem_v_prompt, kb, vb = _kv_call(mem_prompt, norm_mem, w_kv_b)

    pool_mask = jnp.concatenate(
        [jnp.broadcast_to((jnp.arange(POOL_BUF) >= POOL_BUF + 1 - w).astype(F32)[:, None], (POOL_BUF, GROUP))
         for w in POOL_WINDOWS], axis=1)
    time_major = lambda a: jnp.transpose(a, (0, 2, 1, 3))
    za, sha = _state_call(time_major(state_conv_a), conv_a_w[:, :KA - 1], 2, "state_conv_a")
    zb, shb = _state_call(time_major(state_conv_b), conv_b_w[:, :KB - 1], 6, "state_conv_b")
    sp, shp = _state_call(time_major(state_pool), jnp.broadcast_to(pool_mask, (DEPTH, POOL_BUF, D)),
                          5, "state_pool")
    x = x_prompt
    xs = x_sample.reshape(ns, D)
    tails_a, tails_b, tails_p = [], [], []
    for l in range(DEPTH):
        layer = (norm_mix[l], conv_a_w[l], conv_b_w[l], conv_b_bias[l], ln_b_gain[l], ln_b_bias[l],
                 pool_scale[l], gate_bias[l])
        x, ta, tb, tp = _mix_call(l, x, kb, vb, w_in, w_o, pw_b[l], *layer)
        part, q, g3, sha, shb, shp = _mix_sample_call(l, xs, za, zb, sp, w_in, pw_b[l], *layer,
                                                      sha, shb, shp)
        x, ym = _ffn_call(l, x.reshape(nb * seq, D), norm_ffn[l], w_ff1, w_ff2, norm_final,
                          final_norm=(l == DEPTH - 1), attn=(q, cache_mem_k, cache_mem_v),
                          name=f"ffn_prompt_l{l}")
        x = x.reshape(nb, seq, D)
        xs = _ffn_call(l, xs, norm_ffn[l], w_ff1, w_ff2, norm_final, final_norm=(l == DEPTH - 1),
                       merge=(part, g3, ym.reshape(ns, D), w_o), name=f"ffn_sample_l{l}")
        tails_a.append(ta[:, HALO_A - (KA - 1):])
        tails_b.append(tb[:, HALO_B - (KB - 1):])
        tails_p.append(tp[:, HALO_P - POOL_BUF:])
    y_prompt = x
    conv_a_prompt = jnp.stack(tails_a)
    conv_b_prompt = jnp.stack(tails_b)
    pool_prompt = jnp.stack(tails_p)
    y_sample = xs.reshape(ns, 1, D)
    conv_a_sample = time_major(sha)
    conv_b_sample = time_major(shb)
    pool_sample = time_major(shp)

    return (y_prompt, y_sample, mem_k_prompt, mem_v_prompt, conv_a_prompt, conv_b_prompt,
            pool_prompt, conv_a_sample, conv_b_sample, pool_sample)
```

```python
import functools

import jax
import jax.numpy as jnp
from jax import lax
from jax.experimental import pallas as pl
from jax.experimental.pallas import tpu as pltpu

D = 1024
DEPTH = 2
N_MEM = 256
N_HEADS = 4
HEAD = D // N_HEADS
N_GROUPS = 4
GROUP = D // N_GROUPS
POOL_WINDOWS = (2, 4, 8, 16)
POOL_BUF = 15
KA = 3
KB = 31
D_FF = 4 * D
D_PROJ = 11 * D
PAST_LEN = 16384
EPS = 1e-6

C_HA, C_BA, C_CA, C_GLU_V, C_GLU_G, C_P, C_Q, C_GATE = (
    0, D, 2 * D, 3 * D, 4 * D, 5 * D, 6 * D, 7 * D)

SUBLANES = 8
LANES = 128
VMEM_LIMIT = 60 * 1024 * 1024

T_MIX = 256
T_FFN = 512
HALO_A, HALO_B, HALO_P = 8, 32, 16
STAGE_ROWS, STAGE_COLS = 256, 1024
STAGE_SLOTS = 4

BF16 = jnp.bfloat16
F32 = jnp.float32


def _dot(a, b):
    return jnp.dot(a, b, preferred_element_type=F32)


def _rms_rows(x, g):
    ms = jnp.mean(x * x, axis=-1, keepdims=True)
    return x * lax.rsqrt(ms + EPS) * g


def _stage_weights(jobs, stage_ref, sem_ref):
    chunks = [(src, dst, r0, c0) for src, dst in jobs
              for r0 in range(0, src.shape[0], STAGE_ROWS)
              for c0 in range(0, src.shape[1], STAGE_COLS)]
    ahead = STAGE_SLOTS - 1

    def copy(i):
        src, _, r0, c0 = chunks[i]
        return pltpu.make_async_copy(src.at[pl.ds(r0, STAGE_ROWS), pl.ds(c0, STAGE_COLS)],
                                     stage_ref.at[i % STAGE_SLOTS], sem_ref.at[i % STAGE_SLOTS])

    for i in range(min(ahead, len(chunks))):
        copy(i).start()
    for i, (_, dst, r0, c0) in enumerate(chunks):
        if i + ahead < len(chunks):
            copy(i + ahead).start()
        copy(i).wait()
        dst[r0:r0 + STAGE_ROWS, c0:c0 + STAGE_COLS] = stage_ref[i % STAGE_SLOTS].astype(BF16)


def _stage_scratch():
    return [pltpu.VMEM((STAGE_SLOTS, STAGE_ROWS, STAGE_COLS), F32),
            pltpu.SemaphoreType.DMA((STAGE_SLOTS,))]


def _shift_rows(v, s, halo, rows):
    return v[halo - s:halo - s + rows]


def _kv_kernel(mem_ref, g_ref, w_hbm, k_ref, v_ref, kb_ref, vb_ref, w_ref, stage_ref, stage_sem, *, bb):
    @pl.when(pl.program_id(1) == 0)
    def _():
        _stage_weights([(w_hbm.at[pl.program_id(0)], w_ref)], stage_ref, stage_sem)

    mem = mem_ref[...].reshape(bb * N_MEM, D)
    memn = _rms_rows(mem, g_ref[0]).astype(BF16)
    kv = _dot(memn, w_ref[...])
    for r in range(bb):
        k = kv[r * N_MEM:(r + 1) * N_MEM, :D]
        v = kv[r * N_MEM:(r + 1) * N_MEM, D:]
        k_ref[0, r] = k.reshape(N_MEM, N_HEADS, HEAD)
        v_ref[0, r] = v.reshape(N_MEM, N_HEADS, HEAD)
        kb_ref[0, r] = k.astype(BF16)
        vb_ref[0, r] = v.astype(BF16)


def _kv_call(mem, norm_mem, w_kv):
    nb = mem.shape[0]
    bb = 2
    out_f = jax.ShapeDtypeStruct((DEPTH, nb, N_MEM, N_HEADS, HEAD), F32)
    out_b = jax.ShapeDtypeStruct((DEPTH, nb, N_MEM, D), BF16)
    blk = pl.BlockSpec((1, bb, N_MEM, D), lambda l, b: (l, b, 0, 0))
    blk_f = pl.BlockSpec((1, bb, N_MEM, N_HEADS, HEAD), lambda l, b: (l, b, 0, 0, 0))
    return pl.pallas_call(
        functools.partial(_kv_kernel, bb=bb),
        grid=(DEPTH, nb // bb),
        in_specs=[pl.BlockSpec((bb, N_MEM, D), lambda l, b: (b, 0, 0)),
                  pl.BlockSpec((1, 1, D), lambda l, b: (l, 0, 0)),
                  pl.BlockSpec(memory_space=pl.ANY)],
        out_specs=[blk_f, blk_f, blk, blk],
        out_shape=[out_f, out_f, out_b, out_b],
        scratch_shapes=[pltpu.VMEM((D, 2 * D), BF16)] + _stage_scratch(),
        compiler_params=pltpu.CompilerParams(
            dimension_semantics=("arbitrary", "arbitrary"), vmem_limit_bytes=VMEM_LIMIT),
        name="kv_proj",
    )(mem, norm_mem.reshape(DEPTH, 1, D), w_kv)


def _mix_kernel(x_ref, kb_ref, vb_ref, w_in_hbm, w_o_hbm, pw_ref, nmix_ref, caw_ref,
                cbw_ref, cbb_ref, lng_ref, lnb_ref, psc_ref, gb_ref,
                x1_ref, ta_ref, tb_ref, tp_ref,
                w_in_ref, w_o_ref, stage_ref, stage_sem,
                xn_ref, pb_ref, proj_ref, ext_a, ext_b, ext_p, z_ref, mg_ref, mgb_ref, pooled_ref,
                p_ref, *, l):
    T = T_MIX
    j = pl.program_id(1)

    @pl.when((pl.program_id(0) == 0) & (j == 0))
    def _():
        _stage_weights([(w_in_hbm.at[l], w_in_ref), (w_o_hbm.at[l], w_o_ref)], stage_ref, stage_sem)

    @pl.when(j == 0)
    def _():
        ext_a[0:HALO_A, :] = jnp.zeros((HALO_A, D), F32)
        ext_b[0:HALO_B, :] = jnp.zeros((HALO_B, D), F32)
        ext_p[0:HALO_P, :] = jnp.zeros((HALO_P, D), F32)

    g_mix = nmix_ref[...]
    for r0 in range(0, T, 32):
        xn_ref[r0:r0 + 32, :] = _rms_rows(x_ref[0, r0:r0 + 32, :], g_mix).astype(BF16)

    pb_ref[...] = _dot(xn_ref[...], w_in_ref[:, C_GLU_V:C_P])
    P_Q, P_GATE = 3 * D, 4 * D
    proj_ref[:, 0:P_Q] = _dot(xn_ref[...], w_in_ref[:, C_HA:C_GLU_V])
    ext_p[HALO_P:HALO_P + T, :] = _dot(xn_ref[...], w_in_ref[:, C_P:C_Q])
    proj_ref[:, P_Q:P_GATE] = _dot(xn_ref[...], w_in_ref[:, C_Q:C_GATE])
    proj_ref[:, P_GATE:P_GATE + 4 * D] = _dot(xn_ref[...], w_in_ref[:, C_GATE:D_PROJ])

    def gate(i, rows, cols=slice(0, D)):
        c = slice(P_GATE + i * D + cols.start, P_GATE + i * D + cols.stop)
        bc = slice(i * D + cols.start, i * D + cols.stop)
        return jax.nn.sigmoid(proj_ref[rows, c] + gb_ref[:, bc])

    for r0 in range(0, T, 32):
        rows = slice(r0, r0 + 32)
        glu = pb_ref[rows, 0:D] * jax.nn.sigmoid(pb_ref[rows, D:2 * D])
        ext_b[HALO_B + r0:HALO_B + r0 + 32, :] = glu

    R = 128
    for r0 in range(0, T, R):
        for c0 in range(0, D, LANES):
            cols = slice(c0, c0 + LANES)
            acc = None
            for b in range(SUBLANES):
                q = None
                for a in range(4):
                    d = SUBLANES * a + b
                    if d > KB - 1:
                        continue
                    base = HALO_B + r0 - SUBLANES - SUBLANES * a
                    term = ext_b[base:base + R + SUBLANES, cols] * cbw_ref[KB - 1 - d:KB - d, cols]
                    q = term if q is None else q + term
                sh = _shift_rows(q, b, SUBLANES, R)
                acc = sh if acc is None else acc + sh
            z_ref[r0:r0 + R, cols] = acc + cbb_ref[:, cols]
    for r0 in range(0, T, 16):
        rows = slice(r0, r0 + 16)
        z = z_ref[rows, :]
        mu = jnp.mean(z, axis=-1, keepdims=True)
        zc = z - mu
        var = jnp.mean(zc * zc, axis=-1, keepdims=True)
        y = zc * lax.rsqrt(var + EPS) * lng_ref[...] + lnb_ref[...]
        y = y * jax.nn.sigmoid(y)
        mg_ref[rows, :] = gate(1, rows) * y

    for r0 in range(0, T, 32):
        rows = slice(r0, r0 + 32)
        ext_a[HALO_A + r0:HALO_A + r0 + 32, :] = proj_ref[rows, C_CA:C_GLU_V] * proj_ref[rows, C_HA:C_BA]
    for r0 in range(0, T, 32):
        rows = slice(r0, r0 + 32)
        for c0 in range(0, D, 256):
            cols = slice(c0, c0 + 256)
            e = ext_a[r0:r0 + 32 + HALO_A, cols]
            conv = (e[HALO_A:] * caw_ref[2:3, cols]
                    + _shift_rows(e, 1, HALO_A, 32) * caw_ref[1:2, cols]
                    + _shift_rows(e, 2, HALO_A, 32) * caw_ref[0:1, cols])
            y_a = proj_ref[rows, C_BA + c0:C_BA + c0 + 256] * conv
            mg_ref[rows, cols] += gate(0, rows, cols) * y_a

    for r0 in range(0, T, 32):
        pos1 = j * T + r0 + 1 + lax.broadcasted_iota(jnp.int32, (32, 1), 0)
        for g, w in enumerate(POOL_WINDOWS):
            cols = slice(g * GROUP, (g + 1) * GROUP)
            e = ext_p[r0:r0 + 32 + HALO_P, cols]
            s = e
            stride = 1
            while stride < w:
                s = s + _roll_down(s, stride)
                stride *= 2
            cnt = jnp.minimum(pos1, w).astype(F32)
            tok = e[HALO_P:]
            pooled = s[HALO_P:] * (1.0 / cnt) - tok
            pooled_ref[r0:r0 + 32, cols] = pooled.astype(BF16)
    for g in range(N_GROUPS):
        cols = slice(g * GROUP, (g + 1) * GROUP)
        y_c = _dot(pooled_ref[:, cols], pw_ref[g]) * psc_ref[:, cols]
        for r0 in range(0, T, 64):
            rows = slice(r0, r0 + 64)
            mg_ref[rows, cols] += gate(2, rows, cols) * y_c[r0:r0 + 64]

    for h in range(N_HEADS):
        cols = slice(h * HEAD, (h + 1) * HEAD)
        qh = proj_ref[:, P_Q + h * HEAD:P_Q + (h + 1) * HEAD].astype(BF16)
        s = lax.dot_general(qh, kb_ref[0, 0, :, cols], (((1,), (1,)), ((), ())),
                            preferred_element_type=F32) * (HEAD ** -0.5)
        for r0 in range(0, T, 64):
            sc = s[r0:r0 + 64]
            m = jnp.max(sc, axis=-1, keepdims=True)
            e = jnp.exp(sc - m)
            p = e * (1.0 / jnp.sum(e, axis=-1, keepdims=True))
            p_ref[r0:r0 + 64, :] = p.astype(BF16)
        y_m = _dot(p_ref[...], vb_ref[0, 0, :, cols])
        for r0 in range(0, T, 64):
            rows = slice(r0, r0 + 64)
            mg_ref[rows, cols] += gate(3, rows, cols) * y_m[r0:r0 + 64]

    for r0 in range(0, T, 32):
        mgb_ref[r0:r0 + 32, :] = mg_ref[r0:r0 + 32, :].astype(BF16)
    x1_ref[0] = x_ref[0] + _dot(mgb_ref[...], w_o_ref[...])

    ta = ext_a[T:T + HALO_A, :]
    tb = ext_b[T:T + HALO_B, :]
    tp = ext_p[T:T + HALO_P, :]
    ta_ref[0] = ta
    tb_ref[0] = tb
    tp_ref[0] = tp
    ext_a[0:HALO_A, :] = ta
    ext_b[0:HALO_B, :] = tb
    ext_p[0:HALO_P, :] = tp


def _roll_down(v, s):
    return pltpu.roll(v, s, axis=0)


def _const_spec(shape):
    nd = len(shape)
    return pl.BlockSpec(shape, lambda b, j: (0,) * nd, pipeline_mode=pl.Buffered(1))


def _mix_call(l, x, kb, vb, w_in, w_o, pw_b, nmix, caw, cbw, cbb, lng, lnb, psc, gb):
    nb, seq, _ = x.shape
    T = T_MIX
    vec = lambda a: a.reshape(1, -1)
    kv_spec = pl.BlockSpec((1, 1, N_MEM, D), lambda b, j: (l, b, 0, 0))
    hbm = pl.BlockSpec(memory_space=pl.ANY)
    return pl.pallas_call(
        functools.partial(_mix_kernel, l=l),
        grid=(nb, seq // T),
        in_specs=[pl.BlockSpec((1, T, D), lambda b, j: (b, j, 0)),
                  kv_spec, kv_spec,
                  hbm, hbm, _const_spec((N_GROUPS, GROUP, GROUP)),
                  _const_spec((1, D)), _const_spec((KA, D)), _const_spec((KB, D)),
                  _const_spec((1, D)), _const_spec((1, D)), _const_spec((1, D)), _const_spec((1, D)),
                  _const_spec((1, 4 * D))],
        out_specs=[pl.BlockSpec((1, T, D), lambda b, j: (b, j, 0)),
                   pl.BlockSpec((1, HALO_A, D), lambda b, j: (b, 0, 0)),
                   pl.BlockSpec((1, HALO_B, D), lambda b, j: (b, 0, 0)),
                   pl.BlockSpec((1, HALO_P, D), lambda b, j: (b, 0, 0))],
        out_shape=[jax.ShapeDtypeStruct((nb, seq, D), F32),
                   jax.ShapeDtypeStruct((nb, HALO_A, D), F32),
                   jax.ShapeDtypeStruct((nb, HALO_B, D), F32),
                   jax.ShapeDtypeStruct((nb, HALO_P, D), F32)],
        scratch_shapes=[pltpu.VMEM((D, D_PROJ), BF16),
                        pltpu.VMEM((D, D), BF16)]
        + _stage_scratch()
        + [pltpu.VMEM((T, D), BF16),
           pltpu.VMEM((T, 2 * D), F32),
           pltpu.VMEM((T, 8 * D), F32),
           pltpu.VMEM((HALO_A + T, D), F32),
           pltpu.VMEM((HALO_B + T, D), F32),
           pltpu.VMEM((HALO_P + T, D), F32),
           pltpu.VMEM((T, D), F32),
           pltpu.VMEM((T, D), F32),
           pltpu.VMEM((T, D), BF16),
           pltpu.VMEM((T, D), BF16),
           pltpu.VMEM((T, N_MEM), BF16)],
        compiler_params=pltpu.CompilerParams(
            dimension_semantics=("arbitrary", "arbitrary"), vmem_limit_bytes=VMEM_LIMIT),
        name=f"mix_prompt_l{l}",
    )(x, kb, vb, w_in, w_o, pw_b, vec(nmix), caw, cbw, vec(cbb), vec(lng), vec(lnb), vec(psc),
      vec(gb))


def _ffn_kernel(*refs, l, tile, with_merge, final_norm, attn_bt):
    R = 32
    if with_merge:
        (x_ref, part_ref, g3_ref, ym_ref, w_o_hbm, nffn_ref, w1_hbm, w2_hbm, nfin_ref,
         o_ref, w1_ref, w2_ref, stage_ref, stage_sem, xn_ref, h_ref, w_o_ref, x1_ref) = refs
        jobs = [(w_o_hbm.at[l], w_o_ref)]
    elif attn_bt:
        (x1_ref, nffn_ref, w1_hbm, w2_hbm, nfin_ref, q_ref, k_ref, v_ref,
         o_ref, att_ref, w1_ref, w2_ref, stage_ref, stage_sem, xn_ref, h_ref) = refs
        jobs = []
    else:
        (x1_ref, nffn_ref, w1_hbm, w2_hbm, nfin_ref,
         o_ref, w1_ref, w2_ref, stage_ref, stage_sem, xn_ref, h_ref) = refs
        jobs = []
    jobs += [(w1_hbm.at[l], w1_ref), (w2_hbm.at[l], w2_ref)]

    @pl.when(pl.program_id(0) == 0)
    def _():
        _stage_weights(jobs, stage_ref, stage_sem)

    if attn_bt:
        _decode_attention(q_ref, k_ref, v_ref, att_ref, attn_bt)

    if with_merge:
        for r0 in range(0, tile, R):
            rows = slice(r0, r0 + R)
            xn_ref[rows, :] = (part_ref[rows, :] + g3_ref[rows, :] * ym_ref[rows, :]).astype(BF16)
        x1_ref[...] = x_ref[...] + _dot(xn_ref[...], w_o_ref[...])
    g_ffn = nffn_ref[...]
    for r0 in range(0, tile, R):
        rows = slice(r0, r0 + R)
        xn_ref[rows, :] = _rms_rows(x1_ref[rows, :], g_ffn).astype(BF16)
    for f0 in range(0, D_FF, D):
        h = jnp.maximum(_dot(xn_ref[...], w1_ref[:, f0:f0 + D]), 0.0)
        h_ref[:, f0:f0 + D] = (h * h).astype(BF16)
    y = x1_ref[...] + _dot(h_ref[...], w2_ref[...])
    if final_norm:
        y = _rms_rows(y, nfin_ref[...])
    o_ref[...] = y


def _ffn_call(l, x, nffn, w1, w2, nfin, *, final_norm, merge=None, attn=None, name):
    n = x.shape[0]
    tile = min(T_FFN, n)
    steps = n // tile
    row = pl.BlockSpec((tile, D), lambda i: (i, 0))
    const = lambda shape: pl.BlockSpec(shape, lambda i: (0,) * len(shape), pipeline_mode=pl.Buffered(1))
    hbm = pl.BlockSpec(memory_space=pl.ANY)
    args, specs = [x], [row]
    out_specs, out_shape, attn_bt = row, jax.ShapeDtypeStruct((n, D), F32), 0
    if merge is not None:
        part, g3, ym, w_o = merge
        args += [part, g3, ym, w_o]
        specs += [row, row, row, hbm]
    args += [nffn.reshape(1, D), w1, w2, nfin.reshape(1, D)]
    specs += [const((1, D)), hbm, hbm, const((1, D))]
    if attn is not None:
        q, cache_k, cache_v = attn
        ns = q.shape[0]
        attn_bt = ns // steps
        assert attn_bt * steps == ns
        kv = pl.BlockSpec((1, attn_bt, N_MEM, N_HEADS, HEAD), lambda i: (l, i, 0, 0, 0))
        qo = pl.BlockSpec((attn_bt, N_HEADS, HEAD), lambda i: (i, 0, 0))
        args += [q.reshape(ns, N_HEADS, HEAD), cache_k, cache_v]
        specs += [qo, kv, kv]
        out_specs = [row, qo]
        out_shape = [out_shape, jax.ShapeDtypeStruct((ns, N_HEADS, HEAD), F32)]
    return pl.pallas_call(
        functools.partial(_ffn_kernel, l=l, tile=tile, with_merge=merge is not None,
                          final_norm=final_norm, attn_bt=attn_bt),
        grid=(steps,),
        in_specs=specs,
        out_specs=out_specs,
        out_shape=out_shape,
        scratch_shapes=[pltpu.VMEM((D, D_FF), BF16), pltpu.VMEM((D_FF, D), BF16)]
        + _stage_scratch()
        + [pltpu.VMEM((tile, D), BF16), pltpu.VMEM((tile, D_FF), BF16)]
        + ([pltpu.VMEM((D, D), BF16), pltpu.VMEM((tile, D), F32)] if merge is not None else []),
        compiler_params=pltpu.CompilerParams(
            dimension_semantics=("arbitrary",), vmem_limit_bytes=VMEM_LIMIT),
        name=name,
    )(*args)


def _state_kernel(st_ref, nxt_ref, wt_ref, red_ref, sh_ref, *, kb):
    k = pl.program_id(1)
    acc = None
    for i in range(kb):
        row = st_ref[0, i]
        term = row * wt_ref[0, pl.ds(k * kb + i, 1), :]
        acc = term if acc is None else acc + term
        if i > 0:
            sh_ref[0, i - 1] = row
    sh_ref[0, kb - 1] = nxt_ref[0, 0]

    @pl.when(k == 0)
    def _():
        red_ref[0] = acc

    @pl.when(k > 0)
    def _():
        red_ref[0] += acc


def _state_call(state_t, wt, kb, name):
    _, k_len, n, _ = state_t.shape
    assert k_len % kb == 0
    blk = pl.BlockSpec((1, kb, n, D), lambda l, k: (l, k, 0, 0))
    return pl.pallas_call(
        functools.partial(_state_kernel, kb=kb),
        grid=(DEPTH, k_len // kb),
        in_specs=[blk,
                  pl.BlockSpec((1, 1, n, D), lambda l, k: (l, jnp.minimum((k + 1) * kb, k_len - 1), 0, 0)),
                  pl.BlockSpec((1, k_len, D), lambda l, k: (l, 0, 0))],
        out_specs=[pl.BlockSpec((1, n, D), lambda l, k: (l, 0, 0)), blk],
        out_shape=[jax.ShapeDtypeStruct((DEPTH, n, D), F32),
                   jax.ShapeDtypeStruct(state_t.shape, F32)],
        compiler_params=pltpu.CompilerParams(
            dimension_semantics=("arbitrary", "arbitrary"), vmem_limit_bytes=VMEM_LIMIT),
        name=name,
    )(state_t, state_t, wt)


def _mix_sample_kernel(x_ref, za_ref, zb_ref, sp_ref, w_in_hbm, pw_ref, nmix_ref, caw_ref, cbw_ref,
                       cbb_ref, lng_ref, lnb_ref, psc_ref, gb_ref, sha_in, shb_in, shp_in,
                       part_ref, q_ref, g3_ref, u_ref, glu_ref, pin_ref,
                       w_in_ref, stage_ref, stage_sem, xn_ref, proj_ref, pooled_ref, *, l, n):
    del sha_in, shb_in, shp_in
    _stage_weights([(w_in_hbm.at[l], w_in_ref)], stage_ref, stage_sem)
    R = 32
    g_mix = nmix_ref[...]
    for r0 in range(0, n, R):
        xn_ref[r0:r0 + R, :] = _rms_rows(x_ref[r0:r0 + R, :], g_mix).astype(BF16)
    for c0 in range(0, D_PROJ, D):
        proj_ref[:, c0:c0 + D] = _dot(xn_ref[...], w_in_ref[:, c0:c0 + D])

    def gate(i, rows, cols=slice(0, D)):
        c = slice(C_GATE + i * D + cols.start, C_GATE + i * D + cols.stop)
        bc = slice(i * D + cols.start, i * D + cols.stop)
        return jax.nn.sigmoid(proj_ref[rows, c] + gb_ref[:, bc])

    cnt = [float(min(PAST_LEN + 1, w)) for w in POOL_WINDOWS]
    for r0 in range(0, n, R):
        rows = slice(r0, r0 + R)
        u = proj_ref[rows, C_CA:C_GLU_V] * proj_ref[rows, C_HA:C_BA]
        u_ref[0, 0, rows, :] = u
        y_a = proj_ref[rows, C_BA:C_CA] * (za_ref[0, rows, :] + u * caw_ref[KA - 1:KA, :])
        merged = gate(0, rows) * y_a
        glu = proj_ref[rows, C_GLU_V:C_GLU_G] * jax.nn.sigmoid(proj_ref[rows, C_GLU_G:C_P])
        glu_ref[0, 0, rows, :] = glu
        z = zb_ref[0, rows, :] + glu * cbw_ref[KB - 1:KB, :] + cbb_ref[...]
        mu = jnp.mean(z, axis=-1, keepdims=True)
        zc = z - mu
        var = jnp.mean(zc * zc, axis=-1, keepdims=True)
        y = zc * lax.rsqrt(var + EPS) * lng_ref[...] + lnb_ref[...]
        y = y * jax.nn.sigmoid(y)
        merged = merged + gate(1, rows) * y
        part_ref[rows, :] = merged
        p_in = proj_ref[rows, C_P:C_Q]
        pin_ref[0, 0, rows, :] = p_in
        s = sp_ref[0, rows, :] + p_in
        for g in range(N_GROUPS):
            cols = slice(g * GROUP, (g + 1) * GROUP)
            pooled_ref[rows, cols] = (s[:, cols] * (1.0 / cnt[g]) - p_in[:, cols]).astype(BF16)
        q_ref[rows, :] = proj_ref[rows, C_Q:C_GATE]
        g3_ref[rows, :] = gate(3, rows)
    for g in range(N_GROUPS):
        cols = slice(g * GROUP, (g + 1) * GROUP)
        y_c = _dot(pooled_ref[:, cols], pw_ref[g]) * psc_ref[:, cols]
        for r0 in range(0, n, R):
            rows = slice(r0, r0 + R)
            part_ref[rows, cols] += gate(2, rows, cols) * y_c[r0:r0 + R]


def _mix_sample_call(l, x, za, zb, sp, w_in, pw_b, nmix, caw, cbw, cbb, lng, lnb, psc, gb,
                     sha, shb, shp):
    n = x.shape[0]
    vec = lambda a: a.reshape(1, -1)
    const = lambda shape: pl.BlockSpec(shape, lambda i: (0,) * len(shape), pipeline_mode=pl.Buffered(1))
    lay = pl.BlockSpec((1, n, D), lambda i: (l, 0, 0))
    row = pl.BlockSpec((n, D), lambda i: (0, 0))
    out = jax.ShapeDtypeStruct((n, D), F32)
    states = (sha, shb, shp)
    last_slot = lambda st: pl.BlockSpec((1, 1, n, D), lambda i: (l, st.shape[1] - 1, 0, 0))
    n_in = 14
    return pl.pallas_call(
        functools.partial(_mix_sample_kernel, l=l, n=n),
        grid=(1,),
        in_specs=[row, lay, lay, lay,
                  pl.BlockSpec(memory_space=pl.ANY), const((N_GROUPS, GROUP, GROUP)),
                  const((1, D)), const((KA, D)), const((KB, D)),
                  const((1, D)), const((1, D)), const((1, D)), const((1, D)), const((1, 4 * D))]
        + [pl.BlockSpec(memory_space=pl.ANY)] * 3,
        out_specs=[row] * 3 + [last_slot(st) for st in states],
        out_shape=[out] * 3 + [jax.ShapeDtypeStruct(st.shape, F32) for st in states],
        input_output_aliases={n_in: 3, n_in + 1: 4, n_in + 2: 5},
        scratch_shapes=[pltpu.VMEM((D, D_PROJ), BF16)] + _stage_scratch()
        + [pltpu.VMEM((n, D), BF16), pltpu.VMEM((n, D_PROJ), F32), pltpu.VMEM((n, D), BF16)],
        compiler_params=pltpu.CompilerParams(
            dimension_semantics=("arbitrary",), vmem_limit_bytes=VMEM_LIMIT),
        name=f"mix_sample_l{l}",
    )(x, za, zb, sp, w_in, pw_b, vec(nmix), caw, cbw, vec(cbb), vec(lng), vec(lnb), vec(psc), vec(gb),
      sha, shb, shp)


def _decode_attention(q_ref, k_ref, v_ref, o_ref, bt):
    half = N_MEM // 2
    pair = lambda ref, i: jnp.concatenate([ref[0, i, 0:half], ref[0, i, half:N_MEM]], axis=1)
    both = lambda a: jnp.concatenate([a, a], axis=1)
    for i in range(bt):
        q = q_ref[i] * (HEAD ** -0.5)
        q8 = jnp.concatenate([q, q], axis=0)
        s = jnp.sum(pair(k_ref, i) * q8[None], axis=-1, keepdims=True)
        m8 = jnp.max(s, axis=0, keepdims=True)
        m = both(jnp.maximum(m8[:, 0:N_HEADS], m8[:, N_HEADS:]))
        e = jnp.exp(s - m)
        d8 = jnp.sum(e, axis=0)
        y8 = jnp.sum(e * pair(v_ref, i), axis=0)
        o_ref[i] = (y8[0:N_HEADS] + y8[N_HEADS:]) * (1.0 / (d8[0:N_HEADS] + d8[N_HEADS:]))


def kernel(x_prompt, x_sample, mem_prompt, cache_mem_k, cache_mem_v, state_conv_a, state_conv_b,
           state_pool, norm_mix, norm_mem, w_kv, w_in, conv_a_w, conv_b_w, conv_b_bias, ln_b_gain,
           ln_b_bias, pool_w, pool_scale, gate_bias, w_o, norm_ffn, w_ff1, w_ff2, norm_final):
    nb, seq, _ = x_prompt.shape
    ns = x_sample.shape[0]

    pw_b = pool_w.astype(BF16)

    mem_k_prompt, mem_v_prompt, kb, vb = _kv_call(mem_prompt, norm_mem, w_kv)

    pool_mask = jnp.concatenate(
        [jnp.broadcast_to((jnp.arange(POOL_BUF) >= POOL_BUF + 1 - w).astype(F32)[:, None], (POOL_BUF, GROUP))
         for w in POOL_WINDOWS], axis=1)
    time_major = lambda a: jnp.transpose(a, (0, 2, 1, 3))
    za, sha = _state_call(time_major(state_conv_a), conv_a_w[:, :KA - 1], 2, "state_conv_a")
    zb, shb = _state_call(time_major(state_conv_b), conv_b_w[:, :KB - 1], 6, "state_conv_b")
    sp, shp = _state_call(time_major(state_pool), jnp.broadcast_to(pool_mask, (DEPTH, POOL_BUF, D)),
                          5, "state_pool")
    x = x_prompt
    xs = x_sample.reshape(ns, D)
    tails_a, tails_b, tails_p = [], [], []
    for l in range(DEPTH):
        layer = (norm_mix[l], conv_a_w[l], conv_b_w[l], conv_b_bias[l], ln_b_gain[l], ln_b_bias[l],
                 pool_scale[l], gate_bias[l])
        x, ta, tb, tp = _mix_call(l, x, kb, vb, w_in, w_o, pw_b[l], *layer)
        part, q, g3, sha, shb, shp = _mix_sample_call(l, xs, za, zb, sp, w_in, pw_b[l], *layer,
                                                      sha, shb, shp)
        x, ym = _ffn_call(l, x.reshape(nb * seq, D), norm_ffn[l], w_ff1, w_ff2, norm_final,
                          final_norm=(l == DEPTH - 1), attn=(q, cache_mem_k, cache_mem_v),
                          name=f"ffn_prompt_l{l}")
        x = x.reshape(nb, seq, D)
        xs = _ffn_call(l, xs, norm_ffn[l], w_ff1, w_ff2, norm_final, final_norm=(l == DEPTH - 1),
                       merge=(part, g3, ym.reshape(ns, D), w_o), name=f"ffn_sample_l{l}")
        tails_a.append(ta[:, HALO_A - (KA - 1):])
        tails_b.append(tb[:, HALO_B - (KB - 1):])
        tails_p.append(tp[:, HALO_P - POOL_BUF:])
    y_prompt = x
    conv_a_prompt = jnp.stack(tails_a)
    conv_b_prompt = jnp.stack(tails_b)
    pool_prompt = jnp.stack(tails_p)
    y_sample = xs.reshape(ns, 1, D)
    conv_a_sample = time_major(sha)
    conv_b_sample = time_major(shb)
    pool_sample = time_major(shp)

    return (y_prompt, y_sample, mem_k_prompt, mem_v_prompt, conv_a_prompt, conv_b_prompt,
            pool_prompt, conv_a_sample, conv_b_sample, pool_sample)
```
